```python
import math
import jax, jax.numpy as jnp
from jax import lax
import numpy as np

D_MODEL = 1024
BATCH = 1
SEQ = 16384
DEPTH = 4
DEC_BATCH = 8
DEC_SEQ = 2048
PAST_LEN = 128

GRID_W = 64
HEAD_DIM = 64
A_HEADS = 4
A_KV_HEADS = 2
B_HEADS = 4
B_KV_HEADS = 2
WINDOW = 128
BAND_BLOCK = 128
C_HEADS = 4
C_NOPE = 64
C_ROPE = 32
C_V = 64
C_Q_LORA = 384
C_KV_LORA = 256
D_HEADS = 4
D_HEAD_DIM = 32
N_BRANCH = 4
BRANCH_W = 256
D_FF = 2816
Q_BLOCK = 128
ROPE_THETA = 10000.0
EPS = 1e-6
NEG_BIG = -1e30
PROJ_SIZES = (A_HEADS * HEAD_DIM, A_KV_HEADS * HEAD_DIM, A_KV_HEADS * HEAD_DIM,
              B_HEADS * HEAD_DIM, B_KV_HEADS * HEAD_DIM, B_KV_HEADS * HEAD_DIM,
              C_Q_LORA, C_KV_LORA, C_ROPE,
              2 * D_HEADS * D_HEAD_DIM, 2 * D_HEADS * D_HEAD_DIM, 2 * D_HEADS * D_HEAD_DIM,
              N_BRANCH * D_MODEL)
PROJ_COLS = sum(PROJ_SIZES)

kernel_name = "hybrid_gated_parallel_encoder"


def rms_norm(x, g):
    xf = x.astype(jnp.float32)
    y = xf * lax.rsqrt(jnp.mean(xf * xf, axis=-1, keepdims=True) + EPS)
    return (y * g.astype(jnp.float32)).astype(x.dtype)


def rope_cos_sin(pos, dim):
    inv = 1.0 / (ROPE_THETA ** (jnp.arange(0, dim, 2, dtype=jnp.float32) / dim))
    ang = pos[:, None] * inv[None, :]
    ang = jnp.concatenate([ang, ang], axis=-1)
    return jnp.cos(ang), jnp.sin(ang)


def rotate_half(x):
    x1, x2 = jnp.split(x, 2, axis=-1)
    return jnp.concatenate([-x2, x1], axis=-1)


def apply_rope(x, cos, sin):
    c = cos[:, None, :].astype(x.dtype)
    s = sin[:, None, :].astype(x.dtype)
    return x * c + rotate_half(x) * s


def axial_rope(x, row_cs, col_cs):
    half = x.shape[-1] // 2
    return jnp.concatenate([apply_rope(x[..., :half], *row_cs),
                            apply_rope(x[..., half:], *col_cs)], axis=-1)


def sweep_query_blocks(fn, q):
    b, s = q.shape[0], q.shape[1]
    nb = s // Q_BLOCK
    qb = jnp.moveaxis(q.reshape((b, nb, Q_BLOCK) + q.shape[2:]), 1, 0)
    out = jnp.moveaxis(lax.map(fn, qb), 0, 1)
    return out.reshape((b, s) + out.shape[3:])


def dense_gqa(q, k, v, scale):
    def block(qb):
        s = jnp.einsum('bqhgd,bkhd->bhgqk', qb, k).astype(jnp.float32) * scale
        p = jax.nn.softmax(s, axis=-1).astype(v.dtype)
        return jnp.einsum('bhgqk,bkhd->bqhgd', p, v)
    return sweep_query_blocks(block, q)


def banded_sink_gqa(q, k, v, sink, scale):
    b, s, hkv, g, d = q.shape
    nb = s // BAND_BLOCK
    qb = q.reshape(b, nb, BAND_BLOCK, hkv, g, d)
    pad = ((0, 0), (BAND_BLOCK, BAND_BLOCK), (0, 0), (0, 0))
    kp = jnp.pad(k, pad).reshape(b, nb + 2, BAND_BLOCK, hkv, d)
    vp = jnp.pad(v, pad).reshape(b, nb + 2, BAND_BLOCK, hkv, v.shape[-1])
    kw = jnp.concatenate([kp[:, :-2], kp[:, 1:-1], kp[:, 2:]], axis=2)
    vw = jnp.concatenate([vp[:, :-2], vp[:, 1:-1], vp[:, 2:]], axis=2)
    sc = jnp.einsum('bnqhgd,bnkhd->bnhgqk', qb, kw).astype(jnp.float32) * scale
    qpos = jnp.arange(BAND_BLOCK)[:, None]
    kpos = jnp.arange(3 * BAND_BLOCK)[None, :] - BAND_BLOCK
    in_window = jnp.abs(kpos - qpos) <= WINDOW
    abs_k = jnp.arange(nb)[:, None] * BAND_BLOCK + kpos
    in_range = (abs_k >= 0) & (abs_k < s)
    valid = in_window[None, :, :] & in_range[:, None, :]
    sc = jnp.where(valid[None, :, None, None, :, :], sc, NEG_BIG)
    sink_col = jnp.broadcast_to(sink.reshape(hkv, g)[None, None, :, :, None, None].astype(jnp.float32),
                                sc.shape[:-1] + (1,))
    p = jax.nn.softmax(jnp.concatenate([sc, sink_col], axis=-1), axis=-1)[..., :-1]
    o = jnp.einsum('bnhgqk,bnkhd->bnqhgd', p.astype(v.dtype), vw)
    return o.reshape(b, s, hkv, g, v.shape[-1])


def differential_attention(q, k, v, lam, scale):
    def block(qb):
        sc = jnp.einsum('bqhcd,bkhcd->bhcqk', qb, k).astype(jnp.float32) * scale
        p = jax.nn.softmax(sc, axis=-1)
        a = (p[:, :, 0] - lam * p[:, :, 1]).astype(v.dtype)
        return jnp.einsum('bhqk,bkhd->bqhd', a, v)
    return sweep_query_blocks(block, q)


def swiglu(x, wi, wo):
    gate, up = jnp.split(x @ wi, 2, axis=-1)
    return (jax.nn.silu(gate) * up) @ wo


def token_mixing(xn, w_in, a_qk_g, b_sink, c_q_g, c_kv_g, c_w_uq, c_w_ukv,
                 d_lam, d_sub_g, w_branch, w_out, lambda_init, ropes):
    row_cs, col_cs, rope64, rope32 = ropes
    b, s, _ = xn.shape
    proj = xn @ w_in
    offs = np.cumsum(PROJ_SIZES)[:-1].tolist()
    (aq, ak, av, bq, bk, bv, cq, ckv, ckpe, dq, dk, dv, gl) = jnp.split(proj, offs, axis=-1)

    aq = axial_rope(rms_norm(aq.reshape(b, s, A_HEADS, HEAD_DIM), a_qk_g[0]), row_cs, col_cs)
    ak = axial_rope(rms_norm(ak.reshape(b, s, A_KV_HEADS, HEAD_DIM), a_qk_g[1]), row_cs, col_cs)
    av = av.reshape(b, s, A_KV_HEADS, HEAD_DIM)
    oa = dense_gqa(aq.reshape(b, s, A_KV_HEADS, A_HEADS // A_KV_HEADS, HEAD_DIM), ak, av,
                   HEAD_DIM ** -0.5)

    bq = apply_rope(bq.reshape(b, s, B_HEADS, HEAD_DIM), *rope64)
    bk = apply_rope(bk.reshape(b, s, B_KV_HEADS, HEAD_DIM), *rope64)
    bv = bv.reshape(b, s, B_KV_HEADS, HEAD_DIM)
    ob = banded_sink_gqa(bq.reshape(b, s, B_KV_HEADS, B_HEADS // B_KV_HEADS, HEAD_DIM), bk, bv,
                         b_sink, HEAD_DIM ** -0.5)

    cqh = (rms_norm(cq, c_q_g) @ c_w_uq).reshape(b, s, C_HEADS, C_NOPE + C_ROPE)
    q_nope, q_pe = cqh[..., :C_NOPE], apply_rope(cqh[..., C_NOPE:], *rope32)
    kvh = (rms_norm(ckv, c_kv_g) @ c_w_ukv).reshape(b, s, C_HEADS, C_NOPE + C_V)
    k_nope, cv = kvh[..., :C_NOPE], kvh[..., C_NOPE:]
    k_pe = apply_rope(ckpe.reshape(b, s, 1, C_ROPE), *rope32)
    cq_full = jnp.concatenate([q_nope, q_pe], axis=-1)
    ck_full = jnp.concatenate([k_nope, jnp.broadcast_to(k_pe, (b, s, C_HEADS, C_ROPE))], axis=-1)
    oc = dense_gqa(cq_full[:, :, :, None, :], ck_full, cv, (C_NOPE + C_ROPE) ** -0.5)

    dq = apply_rope(dq.reshape(b, s, 2 * D_HEADS, D_HEAD_DIM), *rope32).reshape(b, s, D_HEADS, 2, D_HEAD_DIM)
    dk = apply_rope(dk.reshape(b, s, 2 * D_HEADS, D_HEAD_DIM), *rope32).reshape(b, s, D_HEADS, 2, D_HEAD_DIM)
    dv = dv.reshape(b, s, D_HEADS, 2 * D_HEAD_DIM)
    lf = d_lam.astype(jnp.float32)
    lam = jnp.exp(jnp.sum(lf[0] * lf[1])) - jnp.exp(jnp.sum(lf[2] * lf[3])) + lambda_init
    od = differential_attention(dq, dk, dv, lam, D_HEAD_DIM ** -0.5)
    od = rms_norm(od, d_sub_g) * (1.0 - lambda_init)

    branches = jnp.stack([oa.reshape(b, s, BRANCH_W), ob.reshape(b, s, BRANCH_W),
                          oc.reshape(b, s, BRANCH_W), od.reshape(b, s, BRANCH_W)], axis=2)
    gates = jax.nn.sigmoid(gl.reshape(b, s, N_BRANCH, D_MODEL))
    merged = jnp.sum(gates * jnp.einsum('bsnc,ncd->bsnd', branches, w_branch), axis=2)
    return merged @ w_out


def run_trunk(x, norm_g, w_in, a_qk_norm, b_sink, c_q_norm, c_kv_norm, c_w_uq, c_w_ukv,
              d_lambda, d_subln, w_branch, w_out, ffn_wi, ffn_wo):
    s = x.shape[1]
    rows = s // GRID_W
    row = jnp.repeat(jnp.arange(rows, dtype=jnp.float32), GRID_W)
    col = jnp.tile(jnp.arange(GRID_W, dtype=jnp.float32), rows)
    pos = jnp.arange(s, dtype=jnp.float32)
    ropes = (rope_cos_sin(row, HEAD_DIM // 2), rope_cos_sin(col, HEAD_DIM // 2),
             rope_cos_sin(pos, HEAD_DIM), rope_cos_sin(pos, C_ROPE))
    for l in range(DEPTH):
        g = norm_g[l]
        lambda_init = 0.8 - 0.6 * math.exp(-0.3 * l)
        x = x + 0.5 * rms_norm(swiglu(rms_norm(x, g[0]), ffn_wi[l, 0], ffn_wo[l, 0]), g[1])
        mix = token_mixing(rms_norm(x, g[2]), w_in[l], a_qk_norm[l], b_sink[l], c_q_norm[l],
                           c_kv_norm[l], c_w_uq[l], c_w_ukv[l], d_lambda[l], d_subln[l],
                           w_branch[l], w_out[l], lambda_init, ropes)
        x = x + rms_norm(mix, g[3])
        x = x + 0.5 * rms_norm(swiglu(rms_norm(x, g[4]), ffn_wi[l, 1], ffn_wo[l, 1]), g[5])
    return x


def setup_inputs(seed: int = 0) -> dict:
    key = jax.random.key(seed)
    ks = jax.random.split(key, 17)
    f32 = jnp.float32
    nrm = lambda k, shp, sc: jax.random.normal(k, shp, f32) * sc
    return {
        "x_prompt": nrm(ks[0], (BATCH, SEQ, D_MODEL), 1.0),
        "x_sample": nrm(ks[1], (DEC_BATCH, DEC_SEQ, D_MODEL), 1.0),
        "norm_g": 1.0 + nrm(ks[2], (DEPTH, 6, D_MODEL), 0.05),
        "w_in": nrm(ks[3], (DEPTH, D_MODEL, PROJ_COLS), D_MODEL ** -0.5),
        "a_qk_norm": 1.0 + nrm(ks[4], (DEPTH, 2, HEAD_DIM), 0.05),
        "b_sink": nrm(ks[5], (DEPTH, B_HEADS), 0.5),
        "c_q_norm": 1.0 + nrm(ks[6], (DEPTH, C_Q_LORA), 0.05),
        "c_kv_norm": 1.0 + nrm(ks[7], (DEPTH, C_KV_LORA), 0.05),
        "c_w_uq": nrm(ks[8], (DEPTH, C_Q_LORA, C_HEADS * (C_NOPE + C_ROPE)), C_Q_LORA ** -0.5),
        "c_w_ukv": nrm(ks[9], (DEPTH, C_KV_LORA, C_HEADS * (C_NOPE + C_V)), C_KV_LORA ** -0.5),
        "d_lambda": nrm(ks[10], (DEPTH, 4, D_HEAD_DIM), 0.1),
        "d_subln": 1.0 + nrm(ks[11], (DEPTH, 2 * D_HEAD_DIM), 0.05),
        "w_branch": nrm(ks[12], (DEPTH, N_BRANCH, BRANCH_W, D_MODEL), BRANCH_W ** -0.5),
        "w_out": nrm(ks[13], (DEPTH, D_MODEL, D_MODEL), D_MODEL ** -0.5),
        "ffn_wi": nrm(ks[14], (DEPTH, 2, D_MODEL, 2 * D_FF), D_MODEL ** -0.5),
        "ffn_wo": nrm(ks[15], (DEPTH, 2, D_FF, D_MODEL), D_FF ** -0.5),
    }


def reference(x_prompt, x_sample, norm_g, w_in, a_qk_norm, b_sink, c_q_norm, c_kv_norm,
              c_w_uq, c_w_ukv, d_lambda, d_subln, w_branch, w_out, ffn_wi, ffn_wo):
    y_prompt = run_trunk(x_prompt, norm_g, w_in, a_qk_norm, b_sink, c_q_norm, c_kv_norm,
                         c_w_uq, c_w_ukv, d_lambda, d_subln, w_branch, w_out, ffn_wi, ffn_wo)
    y_sample = run_trunk(x_sample, norm_g, w_in, a_qk_norm, b_sink, c_q_norm, c_kv_norm,
                         c_w_uq, c_w_ukv, d_lambda, d_subln, w_branch, w_out, ffn_wi, ffn_wo)
    return (y_prompt, y_sample)
```

```python
import functools
import math

import numpy as np
import jax
import jax.numpy as jnp
from jax import lax
from jax.experimental import pallas as pl
from jax.experimental.pallas import tpu as pltpu

D_MODEL = 1024
GRID_W = 64
HEAD_DIM = 64
WINDOW = 128
C_NOPE = 64
C_ROPE = 32
C_Q_LORA = 384
C_KV_LORA = 256
D_HEAD_DIM = 32
N_BRANCH = 4
BRANCH_W = 256
D_FF = 2816
ROPE_THETA = 10000.0
EPS = 1e-6
NEG_BIG = -1e30
LOG2E = 1.4426950408889634
N_MIX_COLS = 2464

LANES = 128
TM = 512
TQ = 512
TK = 256
FFN_CHUNK = 1408
VMEM_LIMIT = 56 * 1024 * 1024

BF16 = jnp.bfloat16
F32 = jnp.float32


def _dot(a, b):
    return jnp.dot(a, b, preferred_element_type=F32)


def _lanes(gb, n):
    return jnp.tile(gb, (1, n // LANES))


def _rms_rows(x):
    return x * lax.rsqrt(jnp.mean(x * x, axis=0, keepdims=True) + EPS)


def _sigmoid(x):
    return 1.0 / (1.0 + jnp.exp(-x))


def _swap_halves(x, group):
    h = group // 2
    parts = []
    for r in range(0, x.shape[0], group):
        parts += [x[r + h:r + group], x[r:r + h]]
    return jnp.concatenate(parts, axis=0)


def _rope_rows(x, cos, sin_signed, group):
    n = x.shape[0] // cos.shape[0]
    if n > 1:
        cos = jnp.concatenate([cos] * n, axis=0)
        sin_signed = jnp.concatenate([sin_signed] * n, axis=0)
    return x * cos + _swap_halves(x, group) * sin_signed


def _seq_bounds(q0, t_p, s_p, s_s):
    lo_p = (q0 // s_p) * s_p
    lo_s = t_p + ((q0 - t_p) // s_s) * s_s
    in_p = q0 < t_p
    lo = jnp.where(in_p, lo_p, lo_s)
    hi = jnp.where(in_p, lo_p + s_p, lo_s + s_s)
    return lo, hi


def _ffn_kernel(x_ref, gpre_ref, gpost_ref, wi_ref, wo_ref, o_ref):
    x = x_ref[...]
    tm = x.shape[1]
    xn = (_rms_rows(x) * _lanes(gpre_ref[...], tm)).astype(BF16)
    acc = jnp.zeros((D_MODEL, tm), F32)
    for c in range(D_FF // FFN_CHUNK):
        lo = c * FFN_CHUNK
        hg = _dot(wi_ref[lo:lo + FFN_CHUNK, :], xn)
        hu = _dot(wi_ref[D_FF + lo:D_FF + lo + FFN_CHUNK, :], xn)
        a = (hg * _sigmoid(hg) * hu).astype(BF16)
        acc = acc + _dot(wo_ref[:, lo:lo + FFN_CHUNK], a)
    o_ref[...] = x + 0.5 * (_rms_rows(acc) * _lanes(gpost_ref[...], tm))


def _ffn_call(xT, gb, wiT, woT, l, j):
    t = xT.shape[1]
    const = lambda *idx: (lambda i: idx)
    return pl.pallas_call(
        _ffn_kernel,
        grid=(t // TM,),
        in_specs=[
            pl.BlockSpec((D_MODEL, TM), lambda i: (0, i)),
            pl.BlockSpec((None, None, D_MODEL, LANES), const(l, 4 * j, 0, 0)),
            pl.BlockSpec((None, None, D_MODEL, LANES), const(l, 4 * j + 1, 0, 0)),
            pl.BlockSpec((None, None, 2 * D_FF, D_MODEL), const(l, j, 0, 0), pipeline_mode=pl.Buffered(1)),
            pl.BlockSpec((None, None, D_MODEL, D_FF), const(l, j, 0, 0), pipeline_mode=pl.Buffered(1)),
        ],
        out_specs=pl.BlockSpec((D_MODEL, TM), lambda i: (0, i)),
        out_shape=jax.ShapeDtypeStruct(xT.shape, F32),
        compiler_params=pltpu.CompilerParams(dimension_semantics=("arbitrary",), vmem_limit_bytes=VMEM_LIMIT),
        name=f"ffn_{l}_{j}",
    )(xT, gb, gb, wiT, woT)


def _prep_kernel(x_ref, g_ref, win_ref, aqg_ref, akg_ref, cqg_ref, ckvg_ref, wuq_ref, wukv_ref,
                 cos_a_ref, sin_a_ref, cos64_ref, sin64_ref, cos32_ref, sin32_ref,
                 qa_ref, ka_ref, va_ref, qb_ref, kb_ref, vb_ref,
                 qc_ref, kc_ref, vc_ref, qd_ref, kd_ref, vd_ref):
    x = x_ref[...]
    tm = x.shape[1]
    xn = (_rms_rows(x) * _lanes(g_ref[...], tm)).astype(BF16)
    proj = _dot(win_ref[...], xn)
    aq, ak, av = proj[0:256], proj[256:384], proj[384:512]
    bq, bk, bv = proj[512:768], proj[768:896], proj[896:1024]
    cq, ckv, ckpe = proj[1024:1408], proj[1408:1664], proj[1664:1696]
    dq, dk, dv = proj[1696:1952], proj[1952:2208], proj[2208:2464]
    zeros32 = jnp.zeros((32, tm), F32)
    zeros64 = jnp.zeros((64, tm), F32)

    def store_vt(ref, vt):
        vt = vt.astype(BF16)
        for jj in range(tm // TK):
            ref[jj] = vt[:, jj * TK:(jj + 1) * TK]

    def pad_kv(y, kv):
        return [y, zeros64] if kv == 0 else [zeros64, y]

    cos_a, sin_a = cos_a_ref[...], sin_a_ref[...]
    aqg, akg = _lanes(aqg_ref[...], tm), _lanes(akg_ref[...], tm)
    sc64 = HEAD_DIM ** -0.5 * LOG2E
    parts = []
    for h in range(4):
        y = _rms_rows(aq[64 * h:64 * h + 64]) * aqg
        y = _rope_rows(y, cos_a, sin_a, 32) * sc64
        parts += pad_kv(y, h // 2)
    qa_ref[...] = jnp.concatenate(parts, axis=0).astype(BF16)
    parts = []
    for kv in range(2):
        y = _rms_rows(ak[64 * kv:64 * kv + 64]) * akg
        parts.append(_rope_rows(y, cos_a, sin_a, 32))
    ka_ref[...] = jnp.concatenate(parts, axis=0).T.astype(BF16)
    store_vt(va_ref, av)

    cos64, sin64 = cos64_ref[...], sin64_ref[...]
    yb = _rope_rows(bq, cos64, sin64, 64) * sc64
    parts = []
    for h in range(4):
        parts += pad_kv(yb[64 * h:64 * h + 64], h // 2)
    qb_ref[...] = jnp.concatenate(parts, axis=0).astype(BF16)
    kb_ref[...] = _rope_rows(bk, cos64, sin64, 64).T.astype(BF16)
    store_vt(vb_ref, bv)

    cos32, sin32 = cos32_ref[...], sin32_ref[...]
    cqn = (_rms_rows(cq) * _lanes(cqg_ref[...], tm)).astype(BF16)
    cqh = _dot(wuq_ref[...], cqn)
    kvn = (_rms_rows(ckv) * _lanes(ckvg_ref[...], tm)).astype(BF16)
    kvh = _dot(wukv_ref[...], kvn)
    kpe = _rope_rows(ckpe, cos32, sin32, 32)
    sc96 = (C_NOPE + C_ROPE) ** -0.5 * LOG2E
    qparts, kparts, vparts = [], [], []
    for h in range(4):
        qn = cqh[96 * h:96 * h + 64]
        qp = _rope_rows(cqh[96 * h + 64:96 * h + 96], cos32, sin32, 32)
        qparts += [qn * sc96, qp * sc96, zeros32]
        kparts += [kvh[128 * h:128 * h + 64], kpe, zeros32]
        vparts.append(kvh[128 * h + 64:128 * h + 128])
    qc_ref[...] = jnp.concatenate(qparts, axis=0).astype(BF16)
    kc_ref[...] = jnp.concatenate(kparts, axis=0).T.astype(BF16)
    store_vt(vc_ref, jnp.concatenate(vparts, axis=0))

    sc32 = D_HEAD_DIM ** -0.5 * LOG2E
    yd = _rope_rows(dq, cos32, sin32, 32) * sc32
    parts = []
    for m in range(8):
        blk = [zeros32] * 4
        blk[m % 4] = yd[32 * m:32 * m + 32]
        parts += blk
    qd_ref[...] = jnp.concatenate(parts, axis=0).astype(BF16)
    kd_ref[...] = _rope_rows(dk, cos32, sin32, 32).T.astype(BF16)
    store_vt(vd_ref, dv)


def _prep_call(xT, gb, winT, aqkg, cqg, ckvg, wuqT, wukvT, tabs, l):
    t = xT.shape[1]
    nb = t // TK
    const = lambda *idx: (lambda i: idx)
    tok = lambda rows: pl.BlockSpec((rows, TM), lambda i: (0, i))
    nat = lambda cols: pl.BlockSpec((TM, cols), lambda i: (i, 0))
    vts = lambda rows: pl.BlockSpec((TM // TK, rows, TK), lambda i: (i, 0, 0))
    sds = jax.ShapeDtypeStruct
    out_shape = [
        sds((512, t), BF16), sds((t, 128), BF16), sds((nb, 128, TK), BF16),
        sds((512, t), BF16), sds((t, 128), BF16), sds((nb, 128, TK), BF16),
        sds((512, t), BF16), sds((t, 512), BF16), sds((nb, 256, TK), BF16),
        sds((1024, t), BF16), sds((t, 256), BF16), sds((nb, 256, TK), BF16),
    ]
    out_specs = [tok(512), nat(128), vts(128), tok(512), nat(128), vts(128),
                 tok(512), nat(512), vts(256), tok(1024), nat(256), vts(256)]
    in_specs = [
        tok(D_MODEL),
        pl.BlockSpec((None, None, D_MODEL, LANES), const(l, 2, 0, 0)),
        pl.BlockSpec((None, N_MIX_COLS, D_MODEL), const(l, 0, 0), pipeline_mode=pl.Buffered(1)),
        pl.BlockSpec((None, None, HEAD_DIM, LANES), const(l, 0, 0, 0)),
        pl.BlockSpec((None, None, HEAD_DIM, LANES), const(l, 1, 0, 0)),
        pl.BlockSpec((None, C_Q_LORA, LANES), const(l, 0, 0)),
        pl.BlockSpec((None, C_KV_LORA, LANES), const(l, 0, 0)),
        pl.BlockSpec((None, 384, C_Q_LORA), const(l, 0, 0)),
        pl.BlockSpec((None, 512, C_KV_LORA), const(l, 0, 0)),
        tok(64), tok(64), tok(64), tok(64), tok(32), tok(32),
    ]
    return pl.pallas_call(
        _prep_kernel,
        grid=(t // TM,),
        in_specs=in_specs,
        out_specs=out_specs,
        out_shape=out_shape,
        compiler_params=pltpu.CompilerParams(dimension_semantics=("arbitrary",), vmem_limit_bytes=VMEM_LIMIT),
        name=f"prep_{l}",
    )(xT, gb, winT, aqkg, aqkg, cqg, ckvg, wuqT, wukvT, *tabs)


def _dense_attn_kernel(*refs, n_maps, seq, out_scale):
    if n_maps == 2:
        q_ref, k_ref, v_ref, lam_ref, g_ref, o_ref = refs
    else:
        q_ref, k_ref, v_ref, o_ref = refs
    tq = q_ref.shape[1]
    dv = v_ref.shape[1]
    lo, hi = _seq_bounds(pl.program_id(1) * tq, *seq)
    qs = [q_ref[128 * c:128 * (c + 1), :] for c in range(n_maps)]

    def body(kb, carry):
        kblk = k_ref[kb]
        vblk = v_ref[kb]
        new = []
        for c in range(n_maps):
            m, l, acc = carry[c]
            s = _dot(kblk, qs[c])
            m_new = jnp.maximum(m, jnp.max(s, axis=0, keepdims=True))
            alpha = jnp.exp2(m - m_new)
            p = jnp.exp2(s - m_new)
            l = alpha * l + jnp.sum(p, axis=0, keepdims=True)
            acc = alpha * acc + _dot(vblk, p.astype(BF16))
            new.append((m_new, l, acc))
        return tuple(new)

    init = tuple((jnp.full((1, tq), NEG_BIG, F32), jnp.zeros((1, tq), F32), jnp.zeros((dv, tq), F32))
                 for _ in range(n_maps))
    res = lax.fori_loop(lo // TK, hi // TK, body, init)
    outs = [acc / l for (_, l, acc) in res]
    if n_maps == 1:
        o_ref[...] = outs[0].astype(o_ref.dtype)
    else:
        lf = lam_ref[...]
        lam = (jnp.exp(jnp.sum(lf[0:1] * lf[1:2], axis=1, keepdims=True))
               - jnp.exp(jnp.sum(lf[2:3] * lf[3:4], axis=1, keepdims=True)) + (1.0 - out_scale))
        o = outs[0] - lam * outs[1]
        o = _rms_rows(o) * _lanes(g_ref[...], tq) * out_scale
        o_ref[...] = o.astype(o_ref.dtype)


def _dense_attn_call(qT, k3, v3, seq, *, n_heads, n_maps, k_block, v_block, name, extra=(), out_scale=1.0, l=0):
    t = qT.shape[1]
    nb = k3.shape[0]
    dv = HEAD_DIM
    in_specs = [
        pl.BlockSpec((128 * n_maps, TQ), lambda h, i: (h, i)),
        pl.BlockSpec((nb, TK, 128), lambda h, i: (0, 0, k_block(h))),
        pl.BlockSpec((nb, dv, TK), lambda h, i: (0, v_block(h), 0)),
    ]
    if n_maps == 2:
        in_specs += [pl.BlockSpec((None, 4, D_HEAD_DIM), lambda h, i: (l, 0, 0)),
                     pl.BlockSpec((None, 2 * D_HEAD_DIM, LANES), lambda h, i: (l, 0, 0))]
    return pl.pallas_call(
        functools.partial(_dense_attn_kernel, n_maps=n_maps, seq=seq, out_scale=out_scale),
        grid=(n_heads, t // TQ),
        in_specs=in_specs,
        out_specs=pl.BlockSpec((dv, TQ), lambda h, i: (h, i)),
        out_shape=jax.ShapeDtypeStruct((n_heads * dv, t), BF16),
        compiler_params=pltpu.CompilerParams(dimension_semantics=("arbitrary", "arbitrary"),
                                             vmem_limit_bytes=VMEM_LIMIT),
        name=name,
    )(qT, k3, v3, *extra)


def _band_attn_kernel(sink_ref, q_ref, k0_ref, k1_ref, k2_ref, v0_ref, v1_ref, v2_ref, o_ref, *, seq):
    i = pl.program_id(0)
    tq = q_ref.shape[1]
    lo, hi = _seq_bounds(i * tq, *seq)
    qpos = i * tq + lax.broadcasted_iota(jnp.int32, (TK, tq), 1)
    rows = lax.broadcasted_iota(jnp.int32, (TK, tq), 0)
    valid = []
    for j in range(3):
        kpos = (i - 1 + j) * TK + rows
        d = kpos - qpos
        ok = jnp.where(d <= WINDOW, jnp.where(d >= -WINDOW, 1, 0), 0)
        ok = jnp.where(kpos >= lo, jnp.where(kpos < hi, ok, 0), 0)
        valid.append(ok > 0)
    k_refs = (k0_ref, k1_ref, k2_ref)
    v_refs = (v0_ref, v1_ref, v2_ref)
    for h in range(4):
        kv = h // 2
        qh = q_ref[128 * h:128 * h + 128, :]
        s = [jnp.where(valid[j], _dot(k_refs[j][0], qh), NEG_BIG) for j in range(3)]
        sink = sink_ref[h] * LOG2E
        m = jnp.maximum(jnp.maximum(jnp.max(s[0], axis=0, keepdims=True), jnp.max(s[1], axis=0, keepdims=True)),
                        jnp.maximum(jnp.max(s[2], axis=0, keepdims=True), sink))
        l = jnp.exp2(sink - m)
        acc = jnp.zeros((HEAD_DIM, tq), F32)
        for j in range(3):
            p = jnp.exp2(s[j] - m)
            l = l + jnp.sum(p, axis=0, keepdims=True)
            acc = acc + _dot(v_refs[j][0][64 * kv:64 * kv + 64, :], p.astype(BF16))
        o_ref[64 * h:64 * h + 64, :] = (acc / l).astype(o_ref.dtype)


def _band_attn_call(sink, qT, k3, v3, seq, l):
    t = qT.shape[1]
    nb = t // TK
    clamp = lambda b: jnp.clip(b, 0, nb - 1)
    kspec = lambda off: pl.BlockSpec((1, TK, 128), lambda i: (clamp(i + off), 0, 0))
    vspec = lambda off: pl.BlockSpec((1, 128, TK), lambda i: (clamp(i + off), 0, 0))
    return pl.pallas_call(
        functools.partial(_band_attn_kernel, seq=seq),
        grid=(nb,),
        in_specs=[pl.BlockSpec(memory_space=pltpu.SMEM),
                  pl.BlockSpec((512, TK), lambda i: (0, i)),
                  kspec(-1), kspec(0), kspec(1), vspec(-1), vspec(0), vspec(1)],
        out_specs=pl.BlockSpec((256, TK), lambda i: (0, i)),
        out_shape=jax.ShapeDtypeStruct((256, t), BF16),
        compiler_params=pltpu.CompilerParams(dimension_semantics=("arbitrary",), vmem_limit_bytes=VMEM_LIMIT),
        name=f"attn_b_{l}",
    )(sink, qT, k3, k3, k3, v3, v3, v3)


def _merge_kernel(x_ref, oa_ref, ob_ref, oc_ref, od_ref, gpre_ref, gpost_ref, wg_ref, wb_ref, wout_ref, o_ref):
    x = x_ref[...]
    tm = x.shape[1]
    xn = (_rms_rows(x) * _lanes(gpre_ref[...], tm)).astype(BF16)
    merged = jnp.zeros((D_MODEL, tm), F32)
    for n, br_ref in enumerate((oa_ref, ob_ref, oc_ref, od_ref)):
        gl = _dot(wg_ref[n * D_MODEL:(n + 1) * D_MODEL, :], xn)
        bp = _dot(wb_ref[n], br_ref[...])
        merged = merged + _sigmoid(gl) * bp
    mix = _dot(wout_ref[...], merged.astype(BF16))
    o_ref[...] = x + _rms_rows(mix) * _lanes(gpost_ref[...], tm)


def _merge_call(xT, branches, gb, wgT, wbT, woutT, l):
    t = xT.shape[1]
    const = lambda *idx: (lambda i: idx)
    tok = lambda rows: pl.BlockSpec((rows, TM), lambda i: (0, i))
    return pl.pallas_call(
        _merge_kernel,
        grid=(t // TM,),
        in_specs=[
            tok(D_MODEL), tok(BRANCH_W), tok(BRANCH_W), tok(BRANCH_W), tok(BRANCH_W),
            pl.BlockSpec((None, None, D_MODEL, LANES), const(l, 2, 0, 0)),
            pl.BlockSpec((None, None, D_MODEL, LANES), const(l, 3, 0, 0)),
            pl.BlockSpec((None, N_BRANCH * D_MODEL, D_MODEL), const(l, 0, 0), pipeline_mode=pl.Buffered(1)),
            pl.BlockSpec((None, N_BRANCH, D_MODEL, BRANCH_W), const(l, 0, 0, 0), pipeline_mode=pl.Buffered(1)),
            pl.BlockSpec((None, D_MODEL, D_MODEL), const(l, 0, 0), pipeline_mode=pl.Buffered(1)),
        ],
        out_specs=tok(D_MODEL),
        out_shape=jax.ShapeDtypeStruct(xT.shape, F32),
        compiler_params=pltpu.CompilerParams(dimension_semantics=("arbitrary",), vmem_limit_bytes=VMEM_LIMIT),
        name=f"merge_{l}",
    )(xT, *branches, gb, gb, wgT, wbT, woutT)


def _rope_tables(pos):
    def cs(p, dim):
        inv = 1.0 / (ROPE_THETA ** (jnp.arange(0, dim, 2, dtype=F32) / dim))
        ang = p[:, None] * inv[None, :]
        ang = jnp.concatenate([ang, ang], axis=-1)
        sign = jnp.concatenate([-jnp.ones((dim // 2,), F32), jnp.ones((dim // 2,), F32)])
        return jnp.cos(ang).T, (jnp.sin(ang) * sign[None, :]).T
    row = jnp.floor(pos / GRID_W)
    col = pos - row * GRID_W
    cr, sr = cs(row, HEAD_DIM // 2)
    cc, sc = cs(col, HEAD_DIM // 2)
    c64, s64 = cs(pos, HEAD_DIM)
    c32, s32 = cs(pos, C_ROPE)
    return (jnp.concatenate([cr, cc], axis=0), jnp.concatenate([sr, sc], axis=0), c64, s64, c32, s32)


def kernel(x_prompt, x_sample, norm_g, w_in, a_qk_norm, b_sink, c_q_norm, c_kv_norm, c_w_uq, c_w_ukv,
           d_lambda, d_subln, w_branch, w_out, ffn_wi, ffn_wo):
    b_p, s_p, _ = x_prompt.shape
    b_s, s_s, _ = x_sample.shape
    t_p, t_s = b_p * s_p, b_s * s_s
    t = t_p + t_s
    depth = norm_g.shape[0]
    assert t_p % TM == 0 and t_s % TM == 0 and s_p % TQ == 0 and s_s % TQ == 0 and TQ % TK == 0
    seq = (t_p, s_p, s_s)

    xT = jnp.concatenate([x_prompt.reshape(t_p, D_MODEL), x_sample.reshape(t_s, D_MODEL)], axis=0).T

    pos = jnp.concatenate([jnp.tile(jnp.arange(s_p, dtype=F32), b_p), jnp.tile(jnp.arange(s_s, dtype=F32), b_s)])
    tabs = _rope_tables(pos)

    lane_bcast = lambda g: jnp.broadcast_to(g[..., None].astype(F32), g.shape + (LANES,))
    gb = lane_bcast(norm_g)
    aqkg = lane_bcast(a_qk_norm)
    cqg = lane_bcast(c_q_norm)
    ckvg = lane_bcast(c_kv_norm)
    dsg = lane_bcast(d_subln)
    tr = lambda w: jnp.swapaxes(w, -1, -2).astype(BF16)
    wiT, woT = tr(ffn_wi), tr(ffn_wo)
    winT, wgT = tr(w_in[:, :, :N_MIX_COLS]), tr(w_in[:, :, N_MIX_COLS:])
    wbT, woutT = tr(w_branch), tr(w_out)
    wuqT, wukvT = tr(c_w_uq), tr(c_w_ukv)
    sink = b_sink.astype(F32)
    dlam = d_lambda.astype(F32)

    nb = t // TK
    for l in range(depth):
        lambda_init = 0.8 - 0.6 * math.exp(-0.3 * l)
        xT = _ffn_call(xT, gb, wiT, woT, l, 0)
        (qa, ka, va, qb, kb, vb, qc, kc, vc, qd, kd, vd) = _prep_call(
            xT, gb, winT, aqkg, cqg, ckvg, wuqT, wukvT, tabs, l)
        k3 = lambda k: k.reshape(nb, TK, k.shape[1])
        oa = _dense_attn_call(qa, k3(ka), va, seq, n_heads=4, n_maps=1, k_block=lambda h: 0,
                              v_block=lambda h: h // 2, name=f"attn_a_{l}")
        ob = _band_attn_call(sink[l], qb, k3(kb), vb, seq, l)
        oc = _dense_attn_call(qc, k3(kc), vc, seq, n_heads=4, n_maps=1, k_block=lambda h: h,
                              v_block=lambda h: h, name=f"attn_c_{l}")
        od = _dense_attn_call(qd, k3(kd), vd, seq, n_heads=4, n_maps=2, k_block=lambda h: h // 2,
                              v_block=lambda h: h, name=f"attn_d_{l}", extra=(dlam, dsg),
                              out_scale=1.0 - lambda_init, l=l)
        xT = _merge_call(xT, (oa, ob, oc, od), gb, wgT, wbT, woutT, l)
        xT = _ffn_call(xT, gb, wiT, woT, l, 1)

    y = xT.T
    return (y[:t_p].reshape(b_p, s_p, D_MODEL), y[t_p:].reshape(b_s, s_s, D_MODEL))
```

```python
import functools
import math

import numpy as np
import jax
import jax.numpy as jnp
from jax import lax
from jax.experimental import pallas as pl
from jax.experimental.pallas import tpu as pltpu

D_MODEL = 1024
GRID_W = 64
HEAD_DIM = 64
WINDOW = 128
C_NOPE = 64
C_ROPE = 32
C_Q_LORA = 384
C_KV_LORA = 256
D_HEAD_DIM = 32
N_BRANCH = 4
BRANCH_W = 256
D_FF = 2816
ROPE_THETA = 10000.0
EPS = 1e-6
NEG_BIG = -1e30
LOG2E = 1.4426950408889634
N_MIX_COLS = 2464

LANES = 128
TM = 512
TK = 256
TQC = 256
N_CHAINS = 4
PIPE_G = 2
DV_PAD = 80
FFN_CHUNK = 1408
VMEM_LIMIT = 56 * 1024 * 1024

BF16 = jnp.bfloat16
F32 = jnp.float32


def _dot(a, b):
    return jnp.dot(a, b, preferred_element_type=F32)


def _lanes(gb, n):
    return jnp.tile(gb, (1, n // LANES))


def _rms_rows(x):
    return x * lax.rsqrt(jnp.mean(x * x, axis=0, keepdims=True) + EPS)


def _sigmoid(x):
    return 1.0 / (1.0 + jnp.exp(-x))


def _swap_halves(x, group):
    h = group // 2
    parts = []
    for r in range(0, x.shape[0], group):
        parts += [x[r + h:r + group], x[r:r + h]]
    return jnp.concatenate(parts, axis=0)


def _rope_rows(x, cos, sin_signed, group):
    n = x.shape[0] // cos.shape[0]
    if n > 1:
        cos = jnp.concatenate([cos] * n, axis=0)
        sin_signed = jnp.concatenate([sin_signed] * n, axis=0)
    return x * cos + _swap_halves(x, group) * sin_signed


def _seq_bounds(q0, t_p, s_p, s_s):
    lo_p = (q0 // s_p) * s_p
    lo_s = t_p + ((q0 - t_p) // s_s) * s_s
    in_p = q0 < t_p
    lo = jnp.where(in_p, lo_p, lo_s)
    hi = jnp.where(in_p, lo_p + s_p, lo_s + s_s)
    return lo, hi


def _ffn_kernel(x_ref, gpre_ref, gpost_ref, wi_ref, wo_ref, o_ref):
    x = x_ref[...]
    tm = x.shape[1]
    xn = (_rms_rows(x) * _lanes(gpre_ref[...], tm)).astype(BF16)
    acc = jnp.zeros((D_MODEL, tm), F32)
    for c in range(D_FF // FFN_CHUNK):
        lo = c * FFN_CHUNK
        hg = _dot(wi_ref[lo:lo + FFN_CHUNK, :], xn)
        hu = _dot(wi_ref[D_FF + lo:D_FF + lo + FFN_CHUNK, :], xn)
        a = (hg * _sigmoid(hg) * hu).astype(BF16)
        acc = acc + _dot(wo_ref[:, lo:lo + FFN_CHUNK], a)
    o_ref[...] = x + 0.5 * (_rms_rows(acc) * _lanes(gpost_ref[...], tm))


def _ffn_call(xT, gb, wiT, woT, l, j):
    t = xT.shape[1]
    const = lambda *idx: (lambda i: idx)
    return pl.pallas_call(
        _ffn_kernel,
        grid=(t // TM,),
        in_specs=[
            pl.BlockSpec((D_MODEL, TM), lambda i: (0, i)),
            pl.BlockSpec((None, None, D_MODEL, LANES), const(l, 4 * j, 0, 0)),
            pl.BlockSpec((None, None, D_MODEL, LANES), const(l, 4 * j + 1, 0, 0)),
            pl.BlockSpec((None, None, 2 * D_FF, D_MODEL), const(l, j, 0, 0), pipeline_mode=pl.Buffered(1)),
            pl.BlockSpec((None, None, D_MODEL, D_FF), const(l, j, 0, 0), pipeline_mode=pl.Buffered(1)),
        ],
        out_specs=pl.BlockSpec((D_MODEL, TM), lambda i: (0, i)),
        out_shape=jax.ShapeDtypeStruct(xT.shape, F32),
        compiler_params=pltpu.CompilerParams(dimension_semantics=("arbitrary",), vmem_limit_bytes=VMEM_LIMIT),
        name=f"ffn_{l}_{j}",
    )(xT, gb, gb, wiT, woT)


def _prep_kernel(x_ref, g_ref, win_ref, aqg_ref, akg_ref, cqg_ref, ckvg_ref, wuq_ref, wukv_ref,
                 cos_a_ref, sin_a_ref, cos64_ref, sin64_ref, cos32_ref, sin32_ref,
                 qa_ref, ka_ref, va_ref, qb_ref, kb_ref, vb_ref,
                 qc_ref, kc_ref, vc_ref, qd_ref, kd_ref, vd_ref):
    x = x_ref[...]
    tm = x.shape[1]
    xn = (_rms_rows(x) * _lanes(g_ref[...], tm)).astype(BF16)
    proj = _dot(win_ref[...], xn)
    aq, ak, av = proj[0:256], proj[256:384], proj[384:512]
    bq, bk, bv = proj[512:768], proj[768:896], proj[896:1024]
    cq, ckv, ckpe = proj[1024:1408], proj[1408:1664], proj[1664:1696]
    dq, dk, dv = proj[1696:1952], proj[1952:2208], proj[2208:2464]
    zeros32 = jnp.zeros((32, tm), F32)
    zeros64 = jnp.zeros((64, tm), F32)

    def store_vt(ref, vt):
        vt = vt.astype(BF16)
        for jj in range(tm // TK):
            ref[jj] = vt[:, jj * TK:(jj + 1) * TK]

    ones_rows = jnp.where(lax.broadcasted_iota(jnp.int32, (DV_PAD - HEAD_DIM, tm), 0) == 0, 1.0, 0.0)

    def with_ones(vt):
        parts = []
        for r in range(0, vt.shape[0], HEAD_DIM):
            parts += [vt[r:r + HEAD_DIM], ones_rows]
        return jnp.concatenate(parts, axis=0)

    def pad_kv(y, kv):
        return [y, zeros64] if kv == 0 else [zeros64, y]

    cos_a, sin_a = cos_a_ref[...], sin_a_ref[...]
    aqg, akg = _lanes(aqg_ref[...], tm), _lanes(akg_ref[...], tm)
    sc64 = HEAD_DIM ** -0.5 * LOG2E
    parts = []
    for h in range(4):
        y = _rms_rows(aq[64 * h:64 * h + 64]) * aqg
        y = _rope_rows(y, cos_a, sin_a, 32) * sc64
        parts += pad_kv(y, h // 2)
    qa_ref[...] = jnp.concatenate(parts, axis=0).astype(BF16)
    parts = []
    for kv in range(2):
        y = _rms_rows(ak[64 * kv:64 * kv + 64]) * akg
        parts.append(_rope_rows(y, cos_a, sin_a, 32))
    ka_ref[...] = jnp.concatenate(parts, axis=0).T.astype(BF16)
    store_vt(va_ref, with_ones(av))

    cos64, sin64 = cos64_ref[...], sin64_ref[...]
    yb = _rope_rows(bq, cos64, sin64, 64) * sc64
    parts = []
    for h in range(4):
        parts += pad_kv(yb[64 * h:64 * h + 64], h // 2)
    qb_ref[...] = jnp.concatenate(parts, axis=0).astype(BF16)
    kb_ref[...] = _rope_rows(bk, cos64, sin64, 64).T.astype(BF16)
    store_vt(vb_ref, bv)

    cos32, sin32 = cos32_ref[...], sin32_ref[...]
    cqn = (_rms_rows(cq) * _lanes(cqg_ref[...], tm)).astype(BF16)
    cqh = _dot(wuq_ref[...], cqn)
    kvn = (_rms_rows(ckv) * _lanes(ckvg_ref[...], tm)).astype(BF16)
    kvh = _dot(wukv_ref[...], kvn)
    kpe = _rope_rows(ckpe, cos32, sin32, 32)
    sc96 = (C_NOPE + C_ROPE) ** -0.5 * LOG2E
    qparts, kparts, vparts = [], [], []
    for h in range(4):
        qn = cqh[96 * h:96 * h + 64]
        qp = _rope_rows(cqh[96 * h + 64:96 * h + 96], cos32, sin32, 32)
        qparts += [qn * sc96, qp * sc96, zeros32]
        kparts += [kvh[128 * h:128 * h + 64], kpe, zeros32]
        vparts.append(kvh[128 * h + 64:128 * h + 128])
    qc_ref[...] = jnp.concatenate(qparts, axis=0).astype(BF16)
    kc_ref[...] = jnp.concatenate(kparts, axis=0).T.astype(BF16)
    store_vt(vc_ref, with_ones(jnp.concatenate(vparts, axis=0)))

    sc32 = D_HEAD_DIM ** -0.5 * LOG2E
    yd = _rope_rows(dq, cos32, sin32, 32) * sc32
    parts = []
    for m in range(8):
        blk = [zeros32] * 4
        blk[m % 4] = yd[32 * m:32 * m + 32]
        parts += blk
    qd_ref[...] = jnp.concatenate(parts, axis=0).astype(BF16)
    kd_ref[...] = _rope_rows(dk, cos32, sin32, 32).T.astype(BF16)
    store_vt(vd_ref, with_ones(dv))


def _prep_call(xT, gb, winT, aqkg, cqg, ckvg, wuqT, wukvT, tabs, l):
    t = xT.shape[1]
    nb = t // TK
    const = lambda *idx: (lambda i: idx)
    tok = lambda rows: pl.BlockSpec((rows, TM), lambda i: (0, i))
    nat = lambda cols: pl.BlockSpec((TM, cols), lambda i: (i, 0))
    vts = lambda rows: pl.BlockSpec((TM // TK, rows, TK), lambda i: (i, 0, 0))
    sds = jax.ShapeDtypeStruct
    out_shape = [
        sds((512, t), BF16), sds((t, 128), BF16), sds((nb, 2 * DV_PAD, TK), BF16),
        sds((512, t), BF16), sds((t, 128), BF16), sds((nb, 128, TK), BF16),
        sds((512, t), BF16), sds((t, 512), BF16), sds((nb, 4 * DV_PAD, TK), BF16),
        sds((1024, t), BF16), sds((t, 256), BF16), sds((nb, 4 * DV_PAD, TK), BF16),
    ]
    out_specs = [tok(512), nat(128), vts(2 * DV_PAD), tok(512), nat(128), vts(128),
                 tok(512), nat(512), vts(4 * DV_PAD), tok(1024), nat(256), vts(4 * DV_PAD)]
    in_specs = [
        tok(D_MODEL),
        pl.BlockSpec((None, None, D_MODEL, LANES), const(l, 2, 0, 0)),
        pl.BlockSpec((None, N_MIX_COLS, D_MODEL), const(l, 0, 0), pipeline_mode=pl.Buffered(1)),
        pl.BlockSpec((None, None, HEAD_DIM, LANES), const(l, 0, 0, 0)),
        pl.BlockSpec((None, None, HEAD_DIM, LANES), const(l, 1, 0, 0)),
        pl.BlockSpec((None, C_Q_LORA, LANES), const(l, 0, 0)),
        pl.BlockSpec((None, C_KV_LORA, LANES), const(l, 0, 0)),
        pl.BlockSpec((None, 384, C_Q_LORA), const(l, 0, 0)),
        pl.BlockSpec((None, 512, C_KV_LORA), const(l, 0, 0)),
        tok(64), tok(64), tok(64), tok(64), tok(32), tok(32),
    ]
    return pl.pallas_call(
        _prep_kernel,
        grid=(t // TM,),
        in_specs=in_specs,
        out_specs=out_specs,
        out_shape=out_shape,
        compiler_params=pltpu.CompilerParams(dimension_semantics=("arbitrary",), vmem_limit_bytes=VMEM_LIMIT),
        name=f"prep_{l}",
    )(xT, gb, winT, aqkg, aqkg, cqg, ckvg, wuqT, wukvT, *tabs)


def _dense_attn_kernel(*refs, n_rows, combine, seq, out_scale):
    if combine:
        q_ref, k_ref, v_ref, lam_ref, g_ref, o_ref, s_ref, p_ref, al_ref, m_ref, acc_ref = refs
    else:
        q_ref, k_ref, v_ref, o_ref, s_ref, p_ref, al_ref, m_ref, acc_ref = refs
    tq = q_ref.shape[1]
    n_cg = N_CHAINS // n_rows
    lo, hi = _seq_bounds(pl.program_id(1) * tq, *seq)
    lo_b = lo // TK
    hi_b = hi // TK
    chains = [(r, cg) for r in range(n_rows) for cg in range(n_cg)]
    g2 = PIPE_G

    def q_of(c):
        r, cg = chains[c]
        return q_ref[128 * r:128 * r + 128, TQC * cg:TQC * cg + TQC]

    def pv(kb0, half):
        for g in range(g2):
            vblk = v_ref[jnp.maximum(kb0 + g, lo_b)]
            for c in range(N_CHAINS):
                acc_ref[c] = al_ref[half, g, c] * acc_ref[c] + _dot(vblk, p_ref[half, g, c])

    def mxu_stage(kb_pv, half, kb_s, slot):
        for g in range(g2):
            vblk = v_ref[jnp.maximum(kb_pv + g, lo_b)]
            kblk = k_ref[jnp.minimum(kb_s + g, hi_b - 1)]
            for c in range(N_CHAINS):
                s_ref[slot, g, c] = _dot(kblk, q_of(c))
                acc_ref[c] = al_ref[half, g, c] * acc_ref[c] + _dot(vblk, p_ref[half, g, c])

    def softmax_stage(slot, half):
        for g in range(g2):
            for c in range(N_CHAINS):
                s = s_ref[slot, g, c]
                m = m_ref[c]
                m_new = jnp.maximum(m, jnp.max(s, axis=0, keepdims=True))
                al_ref[half, g, c] = jnp.exp2(m - m_new)
                p_ref[half, g, c] = jnp.exp2((s - m_new).astype(BF16))
                m_ref[c] = m_new

    for c in range(N_CHAINS):
        m_ref[c] = jnp.full((1, TQC), NEG_BIG, F32)
        acc_ref[c] = jnp.zeros((DV_PAD, TQC), F32)
        for half in range(2):
            for g in range(g2):
                p_ref[half, g, c] = jnp.zeros((TK, TQC), BF16)
                al_ref[half, g, c] = jnp.ones((1, TQC), F32)
    for g in range(g2):
        kblk = k_ref[lo_b + g]
        for c in range(N_CHAINS):
            s_ref[0, g, c] = _dot(kblk, q_of(c))

    def body(it, carry):
        kb = lo_b + 2 * g2 * it
        mxu_stage(kb - 2 * g2, 0, kb + g2, 1)
        softmax_stage(0, 0)
        mxu_stage(kb - g2, 1, kb + 2 * g2, 0)
        softmax_stage(1, 1)
        return carry

    lax.fori_loop(0, (hi_b - lo_b) // (2 * g2), body, 0)
    pv(hi_b - 2 * g2, 0)
    pv(hi_b - g2, 1)

    outs = []
    for c in range(N_CHAINS):
        acc = acc_ref[c]
        outs.append(acc[0:HEAD_DIM] / acc[HEAD_DIM:HEAD_DIM + 1])
    if not combine:
        for c, (r, cg) in enumerate(chains):
            o_ref[HEAD_DIM * r:HEAD_DIM * (r + 1), TQC * cg:TQC * (cg + 1)] = outs[c].astype(o_ref.dtype)
    else:
        lf = lam_ref[...]
        lam = (jnp.exp(jnp.sum(lf[0:1] * lf[1:2], axis=1, keepdims=True))
               - jnp.exp(jnp.sum(lf[2:3] * lf[3:4], axis=1, keepdims=True)) + (1.0 - out_scale))
        gain = _lanes(g_ref[...], TQC)
        for cg in range(n_cg):
            o = outs[cg] - lam * outs[n_cg + cg]
            o_ref[:, TQC * cg:TQC * (cg + 1)] = (_rms_rows(o) * gain * out_scale).astype(o_ref.dtype)


def _dense_attn_call(qT, k3, v3, seq, *, n_steps, n_rows, k_block, name, combine=False, extra=(), out_scale=1.0,
                     l=0):
    t = qT.shape[1]
    nb = k3.shape[0]
    tq = TQC * (N_CHAINS // n_rows)
    out_rows = HEAD_DIM if combine else HEAD_DIM * n_rows
    in_specs = [
        pl.BlockSpec((128 * n_rows, tq), lambda h, i: (h, i)),
        pl.BlockSpec((nb, TK, 128), lambda h, i: (0, 0, k_block(h))),
        pl.BlockSpec((nb, DV_PAD, TK), lambda h, i: (0, h, 0)),
    ]
    if combine:
        in_specs += [pl.BlockSpec((None, 4, D_HEAD_DIM), lambda h, i: (l, 0, 0)),
                     pl.BlockSpec((None, 2 * D_HEAD_DIM, LANES), lambda h, i: (l, 0, 0))]
    tiles = (2, PIPE_G, N_CHAINS)
    return pl.pallas_call(
        functools.partial(_dense_attn_kernel, n_rows=n_rows, combine=combine, seq=seq, out_scale=out_scale),
        grid=(n_steps, t // tq),
        in_specs=in_specs,
        out_specs=pl.BlockSpec((out_rows, tq), lambda h, i: (h, i)),
        out_shape=jax.ShapeDtypeStruct((n_steps * out_rows, t), BF16),
        scratch_shapes=[pltpu.VMEM(tiles + (TK, TQC), F32), pltpu.VMEM(tiles + (TK, TQC), BF16),
                        pltpu.VMEM(tiles + (1, TQC), F32), pltpu.VMEM((N_CHAINS, 1, TQC), F32),
                        pltpu.VMEM((N_CHAINS, DV_PAD, TQC), F32)],
        compiler_params=pltpu.CompilerParams(dimension_semantics=("arbitrary", "arbitrary"),
                                             vmem_limit_bytes=VMEM_LIMIT),
        name=name,
    )(qT, k3, v3, *extra)


def _band_attn_kernel(sink_ref, q_ref, k0_ref, k1_ref, k2_ref, v0_ref, v1_ref, v2_ref, o_ref, *, seq):
    i = pl.program_id(0)
    tq = q_ref.shape[1]
    lo, hi = _seq_bounds(i * tq, *seq)
    qpos = i * tq + lax.broadcasted_iota(jnp.int32, (TK, tq), 1)
    rows = lax.broadcasted_iota(jnp.int32, (TK, tq), 0)
    valid = []
    for j in range(3):
        kpos = (i - 1 + j) * TK + rows
        d = kpos - qpos
        ok = jnp.where(d <= WINDOW, jnp.where(d >= -WINDOW, 1, 0), 0)
        ok = jnp.where(kpos >= lo, jnp.where(kpos < hi, ok, 0), 0)
        valid.append(ok > 0)
    k_refs = (k0_ref, k1_ref, k2_ref)
    v_refs = (v0_ref, v1_ref, v2_ref)
    scores = [[jnp.where(valid[j], _dot(k_refs[j][0], q_ref[128 * h:128 * h + 128, :]), NEG_BIG) for j in range(3)]
              for h in range(4)]
    for h in range(4):
        kv = h // 2
        s = scores[h]
        sink = sink_ref[h] * LOG2E
        m = jnp.maximum(jnp.maximum(jnp.max(s[0], axis=0, keepdims=True), jnp.max(s[1], axis=0, keepdims=True)),
                        jnp.maximum(jnp.max(s[2], axis=0, keepdims=True), sink))
        l = jnp.exp2(sink - m)
        acc = jnp.zeros((HEAD_DIM, tq), F32)
        for j in range(3):
            p = jnp.exp2(s[j] - m)
            l = l + jnp.sum(p, axis=0, keepdims=True)
            acc = acc + _dot(v_refs[j][0][64 * kv:64 * kv + 64, :], p.astype(BF16))
        o_ref[64 * h:64 * h + 64, :] = (acc / l).astype(o_ref.dtype)


def _band_attn_call(sink, qT, k3, v3, seq, l):
    t = qT.shape[1]
    nb = t // TK
    clamp = lambda b: jnp.clip(b, 0, nb - 1)
    kspec = lambda off: pl.BlockSpec((1, TK, 128), lambda i: (clamp(i + off), 0, 0))
    vspec = lambda off: pl.BlockSpec((1, 128, TK), lambda i: (clamp(i + off), 0, 0))
    return pl.pallas_call(
        functools.partial(_band_attn_kernel, seq=seq),
        grid=(nb,),
        in_specs=[pl.BlockSpec(memory_space=pltpu.SMEM),
                  pl.BlockSpec((512, TK), lambda i: (0, i)),
                  kspec(-1), kspec(0), kspec(1), vspec(-1), vspec(0), vspec(1)],
        out_specs=pl.BlockSpec((256, TK), lambda i: (0, i)),
        out_shape=jax.ShapeDtypeStruct((256, t), BF16),
        compiler_params=pltpu.CompilerParams(dimension_semantics=("arbitrary",), vmem_limit_bytes=VMEM_LIMIT),
        name=f"attn_b_{l}",
    )(sink, qT, k3, k3, k3, v3, v3, v3)


def _merge_kernel(x_ref, oa_ref, ob_ref, oc_ref, od_ref, gpre_ref, gpost_ref, wg_ref, wb_ref, wout_ref, o_ref):
    x = x_ref[...]
    tm = x.shape[1]
    xn = (_rms_rows(x) * _lanes(gpre_ref[...], tm)).astype(BF16)
    merged = jnp.zeros((D_MODEL, tm), F32)
    for n, br_ref in enumerate((oa_ref, ob_ref, oc_ref, od_ref)):
        gl = _dot(wg_ref[n * D_MODEL:(n + 1) * D_MODEL, :], xn)
        bp = _dot(wb_ref[n], br_ref[...])
        merged = merged + _sigmoid(gl) * bp
    mix = _dot(wout_ref[...], merged.astype(BF16))
    o_ref[...] = x + _rms_rows(mix) * _lanes(gpost_ref[...], tm)


def _merge_call(xT, branches, gb, wgT, wbT, woutT, l):
    t = xT.shape[1]
    const = lambda *idx: (lambda i: idx)
    tok = lambda rows: pl.BlockSpec((rows, TM), lambda i: (0, i))
    return pl.pallas_call(
        _merge_kernel,
        grid=(t // TM,),
        in_specs=[
            tok(D_MODEL), tok(BRANCH_W), tok(BRANCH_W), tok(BRANCH_W), tok(BRANCH_W),
            pl.BlockSpec((None, None, D_MODEL, LANES), const(l, 2, 0, 0)),
            pl.BlockSpec((None, None, D_MODEL, LANES), const(l, 3, 0, 0)),
            pl.BlockSpec((None, N_BRANCH * D_MODEL, D_MODEL), const(l, 0, 0), pipeline_mode=pl.Buffered(1)),
            pl.BlockSpec((None, N_BRANCH, D_MODEL, BRANCH_W), const(l, 0, 0, 0), pipeline_mode=pl.Buffered(1)),
            pl.BlockSpec((None, D_MODEL, D_MODEL), const(l, 0, 0), pipeline_mode=pl.Buffered(1)),
        ],
        out_specs=tok(D_MODEL),
        out_shape=jax.ShapeDtypeStruct(xT.shape, F32),
        compiler_params=pltpu.CompilerParams(dimension_semantics=("arbitrary",), vmem_limit_bytes=VMEM_LIMIT),
        name=f"merge_{l}",
    )(xT, *branches, gb, gb, wgT, wbT, woutT)


def _rope_tables(pos):
    def cs(p, dim):
        inv = 1.0 / (ROPE_THETA ** (jnp.arange(0, dim, 2, dtype=F32) / dim))
        ang = p[:, None] * inv[None, :]
        ang = jnp.concatenate([ang, ang], axis=-1)
        sign = jnp.concatenate([-jnp.ones((dim // 2,), F32), jnp.ones((dim // 2,), F32)])
        return jnp.cos(ang).T, (jnp.sin(ang) * sign[None, :]).T
    row = jnp.floor(pos / GRID_W)
    col = pos - row * GRID_W
    cr, sr = cs(row, HEAD_DIM // 2)
    cc, sc = cs(col, HEAD_DIM // 2)
    c64, s64 = cs(pos, HEAD_DIM)
    c32, s32 = cs(pos, C_ROPE)
    return (jnp.concatenate([cr, cc], axis=0), jnp.concatenate([sr, sc], axis=0), c64, s64, c32, s32)


def kernel(x_prompt, x_sample, norm_g, w_in, a_qk_norm, b_sink, c_q_norm, c_kv_norm, c_w_uq, c_w_ukv,
           d_lambda, d_subln, w_branch, w_out, ffn_wi, ffn_wo):
    b_p, s_p, _ = x_prompt.shape
    b_s, s_s, _ = x_sample.shape
    t_p, t_s = b_p * s_p, b_s * s_s
    t = t_p + t_s
    depth = norm_g.shape[0]
    seq_unit = max(TM, TQC * N_CHAINS, 2 * PIPE_G * TK)
    assert s_p % seq_unit == 0 and s_s % seq_unit == 0
    seq = (t_p, s_p, s_s)

    xT = jnp.concatenate([x_prompt.reshape(t_p, D_MODEL), x_sample.reshape(t_s, D_MODEL)], axis=0).T

    pos = jnp.concatenate([jnp.tile(jnp.arange(s_p, dtype=F32), b_p), jnp.tile(jnp.arange(s_s, dtype=F32), b_s)])
    tabs = _rope_tables(pos)

    lane_bcast = lambda g: jnp.broadcast_to(g[..., None].astype(F32), g.shape + (LANES,))
    gb = lane_bcast(norm_g)
    aqkg = lane_bcast(a_qk_norm)
    cqg = lane_bcast(c_q_norm)
    ckvg = lane_bcast(c_kv_norm)
    dsg = lane_bcast(d_subln)
    tr = lambda w: jnp.swapaxes(w, -1, -2).astype(BF16)
    wiT, woT = tr(ffn_wi), tr(ffn_wo)
    winT, wgT = tr(w_in[:, :, :N_MIX_COLS]), tr(w_in[:, :, N_MIX_COLS:])
    wbT, woutT = tr(w_branch), tr(w_out)
    wuqT, wukvT = tr(c_w_uq), tr(c_w_ukv)
    sink = b_sink.astype(F32)
    dlam = d_lambda.astype(F32)

    nb = t // TK
    for l in range(depth):
        lambda_init = 0.8 - 0.6 * math.exp(-0.3 * l)
        xT = _ffn_call(xT, gb, wiT, woT, l, 0)
        (qa, ka, va, qb, kb, vb, qc, kc, vc, qd, kd, vd) = _prep_call(
            xT, gb, winT, aqkg, cqg, ckvg, wuqT, wukvT, tabs, l)
        k3 = lambda k: k.reshape(nb, TK, k.shape[1])
        oa = _dense_attn_call(qa, k3(ka), va, seq, n_steps=2, n_rows=2, k_block=lambda h: 0, name=f"attn_a_{l}")
        ob = _band_attn_call(sink[l], qb, k3(kb), vb, seq, l)
        oc = _dense_attn_call(qc, k3(kc), vc, seq, n_steps=4, n_rows=1, k_block=lambda h: h, name=f"attn_c_{l}")
        od = _dense_attn_call(qd, k3(kd), vd, seq, n_steps=4, n_rows=2, k_block=lambda h: h // 2,
                              name=f"attn_d_{l}", combine=True, extra=(dlam, dsg),
                              out_scale=1.0 - lambda_init, l=l)
        xT = _merge_call(xT, (oa, ob, oc, od), gb, wgT, wbT, woutT, l)
        xT = _ffn_call(xT, gb, wiT, woT, l, 1)

    y = xT.T
    return (y[:t_p].reshape(b_p, s_p, D_MODEL), y[t_p:].reshape(b_s, s_s, D_MODEL))
```

```python
import functools
import math

import numpy as np
import jax
import jax.numpy as jnp
from jax import lax
from jax.experimental import pallas as pl
from jax.experimental.pallas import tpu as pltpu

D_MODEL = 1024
GRID_W = 64
HEAD_DIM = 64
WINDOW = 128
C_NOPE = 64
C_ROPE = 32
C_Q_LORA = 384
C_KV_LORA = 256
D_HEAD_DIM = 32
N_BRANCH = 4
BRANCH_W = 256
D_FF = 2816
ROPE_THETA = 10000.0
EPS = 1e-6
NEG_BIG = -1e30
LOG2E = 1.4426950408889634
N_MIX_COLS = 2464

LANES = 128
TM = 512
TK = 256
TQC = 256
N_CHAINS = 4
PIPE_G = 2
DV_PAD = 80
FFN_CHUNK = 1408
VMEM_LIMIT = 56 * 1024 * 1024

BF16 = jnp.bfloat16
F32 = jnp.float32


def _dot(a, b):
    return jnp.dot(a, b, preferred_element_type=F32)


def _lanes(gb, n):
    return jnp.tile(gb, (1, n // LANES))


def _rms_rows(x):
    return x * lax.rsqrt(jnp.mean(x * x, axis=0, keepdims=True) + EPS)


def _sigmoid(x):
    return 1.0 / (1.0 + jnp.exp(-x))


def _swap_halves(x, group):
    h = group // 2
    parts = []
    for r in range(0, x.shape[0], group):
        parts += [x[r + h:r + group], x[r:r + h]]
    return jnp.concatenate(parts, axis=0)


def _rope_rows(x, cos, sin_signed, group):
    n = x.shape[0] // cos.shape[0]
    if n > 1:
        cos = jnp.concatenate([cos] * n, axis=0)
        sin_signed = jnp.concatenate([sin_signed] * n, axis=0)
    return x * cos + _swap_halves(x, group) * sin_signed


def _seq_bounds(q0, t_p, s_p, s_s):
    lo_p = (q0 // s_p) * s_p
    lo_s = t_p + ((q0 - t_p) // s_s) * s_s
    in_p = q0 < t_p
    lo = jnp.where(in_p, lo_p, lo_s)
    hi = jnp.where(in_p, lo_p + s_p, lo_s + s_s)
    return lo, hi


def _ffn_kernel(x_ref, gpre_ref, gpost_ref, wi_ref, wo_ref, o_ref):
    x = x_ref[...]
    tm = x.shape[1]
    xn = (_rms_rows(x) * _lanes(gpre_ref[...], tm)).astype(BF16)
    acc = jnp.zeros((D_MODEL, tm), F32)
    for c in range(D_FF // FFN_CHUNK):
        lo = c * FFN_CHUNK
        hg = _dot(wi_ref[lo:lo + FFN_CHUNK, :], xn)
        hu = _dot(wi_ref[D_FF + lo:D_FF + lo + FFN_CHUNK, :], xn)
        a = (hg * _sigmoid(hg) * hu).astype(BF16)
        acc = acc + _dot(wo_ref[:, lo:lo + FFN_CHUNK], a)
    o_ref[...] = x + 0.5 * (_rms_rows(acc) * _lanes(gpost_ref[...], tm))


def _ffn_call(xT, gb, wiT, woT, l, j):
    t = xT.shape[1]
    const = lambda *idx: (lambda i: idx)
    return pl.pallas_call(
        _ffn_kernel,
        grid=(t // TM,),
        in_specs=[
            pl.BlockSpec((D_MODEL, TM), lambda i: (0, i)),
            pl.BlockSpec((None, None, D_MODEL, LANES), const(l, 4 * j, 0, 0)),
            pl.BlockSpec((None, None, D_MODEL, LANES), const(l, 4 * j + 1, 0, 0)),
            pl.BlockSpec((None, None, 2 * D_FF, D_MODEL), const(l, j, 0, 0), pipeline_mode=pl.Buffered(1)),
            pl.BlockSpec((None, None, D_MODEL, D_FF), const(l, j, 0, 0), pipeline_mode=pl.Buffered(1)),
        ],
        out_specs=pl.BlockSpec((D_MODEL, TM), lambda i: (0, i)),
        out_shape=jax.ShapeDtypeStruct(xT.shape, F32),
        compiler_params=pltpu.CompilerParams(dimension_semantics=("arbitrary",), vmem_limit_bytes=VMEM_LIMIT),
        name=f"ffn_{l}_{j}",
    )(xT, gb, gb, wiT, woT)


def _prep_kernel(x_ref, g_ref, win_ref, aqg_ref, akg_ref, cqg_ref, ckvg_ref, wuq_ref, wukv_ref,
                 cos_a_ref, sin_a_ref, cos64_ref, sin64_ref, cos32_ref, sin32_ref,
                 qa_ref, ka_ref, va_ref, qb_ref, kb_ref, vb_ref,
                 qc_ref, kc_ref, vc_ref, qd_ref, kd_ref, vd_ref):
    x = x_ref[...]
    tm = x.shape[1]
    xn = (_rms_rows(x) * _lanes(g_ref[...], tm)).astype(BF16)
    proj = _dot(win_ref[...], xn)
    aq, ak, av = proj[0:256], proj[256:384], proj[384:512]
    bq, bk, bv = proj[512:768], proj[768:896], proj[896:1024]
    cq, ckv, ckpe = proj[1024:1408], proj[1408:1664], proj[1664:1696]
    dq, dk, dv = proj[1696:1952], proj[1952:2208], proj[2208:2464]
    zeros32 = jnp.zeros((32, tm), F32)
    zeros64 = jnp.zeros((64, tm), F32)

    def store_vt(ref, vt):
        vt = vt.astype(BF16)
        for jj in range(tm // TK):
            ref[jj] = vt[:, jj * TK:(jj + 1) * TK]

    ones_rows = jnp.where(lax.broadcasted_iota(jnp.int32, (DV_PAD - HEAD_DIM, tm), 0) == 0, 1.0, 0.0)

    def with_ones(vt):
        parts = []
        for r in range(0, vt.shape[0], HEAD_DIM):
            parts += [vt[r:r + HEAD_DIM], ones_rows]
        return jnp.concatenate(parts, axis=0)

    def pad_kv(y, kv):
        return [y, zeros64] if kv == 0 else [zeros64, y]

    cos_a, sin_a = cos_a_ref[...], sin_a_ref[...]
    aqg, akg = _lanes(aqg_ref[...], tm), _lanes(akg_ref[...], tm)
    sc64 = HEAD_DIM ** -0.5 * LOG2E
    parts = []
    for h in range(4):
        y = _rms_rows(aq[64 * h:64 * h + 64]) * aqg
        y = _rope_rows(y, cos_a, sin_a, 32) * sc64
        parts += pad_kv(y, h // 2)
    qa_ref[...] = jnp.concatenate(parts, axis=0).astype(BF16)
    parts = []
    for kv in range(2):
        y = _rms_rows(ak[64 * kv:64 * kv + 64]) * akg
        parts.append(_rope_rows(y, cos_a, sin_a, 32))
    ka_ref[...] = jnp.concatenate(parts, axis=0).T.astype(BF16)
    store_vt(va_ref, with_ones(av))

    cos64, sin64 = cos64_ref[...], sin64_ref[...]
    yb = _rope_rows(bq, cos64, sin64, 64) * sc64
    parts = []
    for h in range(4):
        parts += pad_kv(yb[64 * h:64 * h + 64], h // 2)
    qb_ref[...] = jnp.concatenate(parts, axis=0).astype(BF16)
    kb_ref[...] = _rope_rows(bk, cos64, sin64, 64).T.astype(BF16)
    store_vt(vb_ref, bv)

    cos32, sin32 = cos32_ref[...], sin32_ref[...]
    cqn = (_rms_rows(cq) * _lanes(cqg_ref[...], tm)).astype(BF16)
    cqh = _dot(wuq_ref[...], cqn)
    kvn = (_rms_rows(ckv) * _lanes(ckvg_ref[...], tm)).astype(BF16)
    kvh = _dot(wukv_ref[...], kvn)
    kpe = _rope_rows(ckpe, cos32, sin32, 32)
    sc96 = (C_NOPE + C_ROPE) ** -0.5 * LOG2E
    qparts, kparts, vparts = [], [], []
    for h in range(4):
        qn = cqh[96 * h:96 * h + 64]
        qp = _rope_rows(cqh[96 * h + 64:96 * h + 96], cos32, sin32, 32)
        qparts += [qn * sc96, qp * sc96, zeros32]
        kparts += [kvh[128 * h:128 * h + 64], kpe, zeros32]
        vparts.append(kvh[128 * h + 64:128 * h + 128])
    qc_ref[...] = jnp.concatenate(qparts, axis=0).astype(BF16)
    kc_ref[...] = jnp.concatenate(kparts, axis=0).T.astype(BF16)
    store_vt(vc_ref, with_ones(jnp.concatenate(vparts, axis=0)))

    sc32 = D_HEAD_DIM ** -0.5 * LOG2E
    yd = _rope_rows(dq, cos32, sin32, 32) * sc32
    parts = []
    for m in range(8):
        blk = [zeros32] * 4
        blk[m % 4] = yd[32 * m:32 * m + 32]
        parts += blk
    qd_ref[...] = jnp.concatenate(parts, axis=0).astype(BF16)
    kd_ref[...] = _rope_rows(dk, cos32, sin32, 32).T.astype(BF16)
    store_vt(vd_ref, with_ones(dv))


def _prep_call(xT, gb, winT, aqkg, cqg, ckvg, wuqT, wukvT, tabs, l):
    t = xT.shape[1]
    nb = t // TK
    const = lambda *idx: (lambda i: idx)
    tok = lambda rows: pl.BlockSpec((rows, TM), lambda i: (0, i))
    nat = lambda cols: pl.BlockSpec((TM, cols), lambda i: (i, 0))
    vts = lambda rows: pl.BlockSpec((TM // TK, rows, TK), lambda i: (i, 0, 0))
    sds = jax.ShapeDtypeStruct
    out_shape = [
        sds((512, t), BF16), sds((t, 128), BF16), sds((nb, 2 * DV_PAD, TK), BF16),
        sds((512, t), BF16), sds((t, 128), BF16), sds((nb, 128, TK), BF16),
        sds((512, t), BF16), sds((t, 512), BF16), sds((nb, 4 * DV_PAD, TK), BF16),
        sds((1024, t), BF16), sds((t, 256), BF16), sds((nb, 4 * DV_PAD, TK), BF16),
    ]
    out_specs = [tok(512), nat(128), vts(2 * DV_PAD), tok(512), nat(128), vts(128),
                 tok(512), nat(512), vts(4 * DV_PAD), tok(1024), nat(256), vts(4 * DV_PAD)]
    in_specs = [
        tok(D_MODEL),
        pl.BlockSpec((None, None, D_MODEL, LANES), const(l, 2, 0, 0)),
        pl.BlockSpec((None, N_MIX_COLS, D_MODEL), const(l, 0, 0), pipeline_mode=pl.Buffered(1)),
        pl.BlockSpec((None, None, HEAD_DIM, LANES), const(l, 0, 0, 0)),
        pl.BlockSpec((None, None, HEAD_DIM, LANES), const(l, 1, 0, 0)),
        pl.BlockSpec((None, C_Q_LORA, LANES), const(l, 0, 0)),
        pl.BlockSpec((None, C_KV_LORA, LANES), const(l, 0, 0)),
        pl.BlockSpec((None, 384, C_Q_LORA), const(l, 0, 0)),
        pl.BlockSpec((None, 512, C_KV_LORA), const(l, 0, 0)),
        tok(64), tok(64), tok(64), tok(64), tok(32), tok(32),
    ]
    return pl.pallas_call(
        _prep_kernel,
        grid=(t // TM,),
        in_specs=in_specs,
        out_specs=out_specs,
        out_shape=out_shape,
        compiler_params=pltpu.CompilerParams(dimension_semantics=("arbitrary",), vmem_limit_bytes=VMEM_LIMIT),
        name=f"prep_{l}",
    )(xT, gb, winT, aqkg, aqkg, cqg, ckvg, wuqT, wukvT, *tabs)


def _dense_attn_kernel(*refs, n_rows, combine, seq, out_scale):
    if combine:
        q_ref, k_ref, v_ref, lam_ref, g_ref, o_ref, s_ref, p_ref, al_ref, m_ref, acc_ref = refs
    else:
        q_ref, k_ref, v_ref, o_ref, s_ref, p_ref, al_ref, m_ref, acc_ref = refs
    tq = q_ref.shape[1]
    n_cg = N_CHAINS // n_rows
    lo, hi = _seq_bounds(pl.program_id(1) * tq, *seq)
    lo_b = lo // TK
    hi_b = hi // TK
    chains = [(r, cg) for r in range(n_rows) for cg in range(n_cg)]
    g2 = PIPE_G

    def q_of(c):
        r, cg = chains[c]
        return q_ref[128 * r:128 * r + 128, TQC * cg:TQC * cg + TQC]

    def mxu_stage(kb_pv, half, kb_s, slot, do_pv=True, do_s=True):
        for g in range(g2):
            if do_pv:
                vblk = v_ref[kb_pv + g]
            if do_s:
                kblk = k_ref[kb_s + g]
            for c in range(N_CHAINS):
                if do_s:
                    s_ref[slot, g, c] = _dot(kblk, q_of(c))
                if do_pv:
                    acc_ref[c] = al_ref[half, g, c] * acc_ref[c] + _dot(vblk, p_ref[half, g, c])

    def softmax_stage(slot, half):
        for g in range(g2):
            for c in range(N_CHAINS):
                s = s_ref[slot, g, c]
                m = m_ref[c]
                m_new = jnp.maximum(m, jnp.max(s, axis=0, keepdims=True))
                al_ref[half, g, c] = jnp.exp2(m - m_new)
                p_ref[half, g, c] = jnp.exp2(s - m_new).astype(BF16)
                m_ref[c] = m_new

    def trip(kb, first, last):
        mxu_stage(kb - 2 * g2, 0, kb + g2, 1, do_pv=not first)
        softmax_stage(0, 0)
        mxu_stage(kb - g2, 1, kb + 2 * g2, 0, do_pv=not first, do_s=not last)
        softmax_stage(1, 1)

    for c in range(N_CHAINS):
        m_ref[c] = jnp.full((1, TQC), NEG_BIG, F32)
        acc_ref[c] = jnp.zeros((DV_PAD, TQC), F32)
    mxu_stage(0, 0, lo_b, 0, do_pv=False)

    n_trips = (hi_b - lo_b) // (2 * g2)

    def single_trip():
        trip(lo_b, True, True)

    def multi_trip():
        trip(lo_b, True, False)

        def body(it, carry):
            trip(lo_b + 2 * g2 * it, False, False)
            return carry

        lax.fori_loop(1, n_trips - 1, body, 0)
        trip(hi_b - 2 * g2, False, True)

    lax.cond(n_trips == 1, single_trip, multi_trip)
    mxu_stage(hi_b - 2 * g2, 0, 0, 0, do_s=False)
    mxu_stage(hi_b - g2, 1, 0, 0, do_s=False)

    outs = []
    for c in range(N_CHAINS):
        acc = acc_ref[c]
        outs.append(acc[0:HEAD_DIM] / acc[HEAD_DIM:HEAD_DIM + 1])
    if not combine:
        for c, (r, cg) in enumerate(chains):
            o_ref[HEAD_DIM * r:HEAD_DIM * (r + 1), TQC * cg:TQC * (cg + 1)] = outs[c].astype(o_ref.dtype)
    else:
        lf = lam_ref[...]
        lam = (jnp.exp(jnp.sum(lf[0:1] * lf[1:2], axis=1, keepdims=True))
               - jnp.exp(jnp.sum(lf[2:3] * lf[3:4], axis=1, keepdims=True)) + (1.0 - out_scale))
        gain = _lanes(g_ref[...], TQC)
        for cg in range(n_cg):
            o = outs[cg] - lam * outs[n_cg + cg]
            o_ref[:, TQC * cg:TQC * (cg + 1)] = (_rms_rows(o) * gain * out_scale).astype(o_ref.dtype)


def _dense_attn_call(qT, k3, v3, seq, *, n_steps, n_rows, k_block, name, combine=False, extra=(), out_scale=1.0,
                     l=0):
    t = qT.shape[1]
    nb = k3.shape[0]
    tq = TQC * (N_CHAINS // n_rows)
    out_rows = HEAD_DIM if combine else HEAD_DIM * n_rows
    in_specs = [
        pl.BlockSpec((128 * n_rows, tq), lambda h, i: (h, i)),
        pl.BlockSpec((nb, TK, 128), lambda h, i: (0, 0, k_block(h))),
        pl.BlockSpec((nb, DV_PAD, TK), lambda h, i: (0, h, 0)),
    ]
    if combine:
        in_specs += [pl.BlockSpec((None, 4, D_HEAD_DIM), lambda h, i: (l, 0, 0)),
                     pl.BlockSpec((None, 2 * D_HEAD_DIM, LANES), lambda h, i: (l, 0, 0))]
    tiles = (2, PIPE_G, N_CHAINS)
    return pl.pallas_call(
        functools.partial(_dense_attn_kernel, n_rows=n_rows, combine=combine, seq=seq, out_scale=out_scale),
        grid=(n_steps, t // tq),
        in_specs=in_specs,
        out_specs=pl.BlockSpec((out_rows, tq), lambda h, i: (h, i)),
        out_shape=jax.ShapeDtypeStruct((n_steps * out_rows, t), BF16),
        scratch_shapes=[pltpu.VMEM(tiles + (TK, TQC), F32), pltpu.VMEM(tiles + (TK, TQC), BF16),
                        pltpu.VMEM(tiles + (1, TQC), F32), pltpu.VMEM((N_CHAINS, 1, TQC), F32),
                        pltpu.VMEM((N_CHAINS, DV_PAD, TQC), F32)],
        compiler_params=pltpu.CompilerParams(dimension_semantics=("arbitrary", "arbitrary"),
                                             vmem_limit_bytes=VMEM_LIMIT),
        name=name,
    )(qT, k3, v3, *extra)


def _band_attn_kernel(sink_ref, q_ref, k0_ref, k1_ref, k2_ref, v0_ref, v1_ref, v2_ref, o_ref, *, seq):
    i = pl.program_id(0)
    tq = q_ref.shape[1]
    lo, hi = _seq_bounds(i * tq, *seq)
    spans = ((TK - WINDOW, TK), (0, TK), (0, WINDOW))
    k_refs = (k0_ref, k1_ref, k2_ref)
    v_refs = (v0_ref, v1_ref, v2_ref)
    valid = []
    for j, (r0, r1) in enumerate(spans):
        shape = (r1 - r0, tq)
        qpos = i * tq + lax.broadcasted_iota(jnp.int32, shape, 1)
        kpos = (i - 1 + j) * TK + r0 + lax.broadcasted_iota(jnp.int32, shape, 0)
        d = kpos - qpos
        ok = jnp.where(d <= WINDOW, jnp.where(d >= -WINDOW, 1, 0), 0)
        ok = jnp.where(kpos >= lo, jnp.where(kpos < hi, ok, 0), 0)
        valid.append(ok > 0)
    scores = [[jnp.where(valid[j], _dot(k_refs[j][0][r0:r1, :], q_ref[128 * h:128 * h + 128, :]), NEG_BIG)
               for j, (r0, r1) in enumerate(spans)] for h in range(4)]
    for h in range(4):
        kv = h // 2
        s = scores[h]
        sink = sink_ref[h] * LOG2E
        m = jnp.maximum(jnp.maximum(jnp.max(s[0], axis=0, keepdims=True), jnp.max(s[1], axis=0, keepdims=True)),
                        jnp.maximum(jnp.max(s[2], axis=0, keepdims=True), sink))
        l = jnp.exp2(sink - m)
        acc = jnp.zeros((HEAD_DIM, tq), F32)
        for j, (r0, r1) in enumerate(spans):
            p = jnp.exp2(s[j] - m)
            l = l + jnp.sum(p, axis=0, keepdims=True)
            acc = acc + _dot(v_refs[j][0][64 * kv:64 * kv + 64, r0:r1], p.astype(BF16))
        o_ref[64 * h:64 * h + 64, :] = (acc / l).astype(o_ref.dtype)


def _band_attn_call(sink, qT, k3, v3, seq, l):
    t = qT.shape[1]
    nb = t // TK
    clamp = lambda b: jnp.clip(b, 0, nb - 1)
    kspec = lambda off: pl.BlockSpec((1, TK, 128), lambda i: (clamp(i + off), 0, 0))
    vspec = lambda off: pl.BlockSpec((1, 128, TK), lambda i: (clamp(i + off), 0, 0))
    return pl.pallas_call(
        functools.partial(_band_attn_kernel, seq=seq),
        grid=(nb,),
        in_specs=[pl.BlockSpec(memory_space=pltpu.SMEM),
                  pl.BlockSpec((512, TK), lambda i: (0, i)),
                  kspec(-1), kspec(0), kspec(1), vspec(-1), vspec(0), vspec(1)],
        out_specs=pl.BlockSpec((256, TK), lambda i: (0, i)),
        out_shape=jax.ShapeDtypeStruct((256, t), BF16),
        compiler_params=pltpu.CompilerParams(dimension_semantics=("arbitrary",), vmem_limit_bytes=VMEM_LIMIT),
        name=f"attn_b_{l}",
    )(sink, qT, k3, k3, k3, v3, v3, v3)


def _merge_kernel(x_ref, oa_ref, ob_ref, oc_ref, od_ref, gpre_ref, gpost_ref, wg_ref, wb_ref, wout_ref, o_ref):
    x = x_ref[...]
    tm = x.shape[1]
    xn = (_rms_rows(x) * _lanes(gpre_ref[...], tm)).astype(BF16)
    merged = jnp.zeros((D_MODEL, tm), F32)
    for n, br_ref in enumerate((oa_ref, ob_ref, oc_ref, od_ref)):
        gl = _dot(wg_ref[n * D_MODEL:(n + 1) * D_MODEL, :], xn)
        bp = _dot(wb_ref[n], br_ref[...])
        merged = merged + _sigmoid(gl) * bp
    mix = _dot(wout_ref[...], merged.astype(BF16))
    o_ref[...] = x + _rms_rows(mix) * _lanes(gpost_ref[...], tm)


def _merge_call(xT, branches, gb, wgT, wbT, woutT, l):
    t = xT.shape[1]
    const = lambda *idx: (lambda i: idx)
    tok = lambda rows: pl.BlockSpec((rows, TM), lambda i: (0, i))
    return pl.pallas_call(
        _merge_kernel,
        grid=(t // TM,),
        in_specs=[
            tok(D_MODEL), tok(BRANCH_W), tok(BRANCH_W), tok(BRANCH_W), tok(BRANCH_W),
            pl.BlockSpec((None, None, D_MODEL, LANES), const(l, 2, 0, 0)),
            pl.BlockSpec((None, None, D_MODEL, LANES), const(l, 3, 0, 0)),
            pl.BlockSpec((None, N_BRANCH * D_MODEL, D_MODEL), const(l, 0, 0), pipeline_mode=pl.Buffered(1)),
            pl.BlockSpec((None, N_BRANCH, D_MODEL, BRANCH_W), const(l, 0, 0, 0), pipeline_mode=pl.Buffered(1)),
            pl.BlockSpec((None, D_MODEL, D_MODEL), const(l, 0, 0), pipeline_mode=pl.Buffered(1)),
        ],
        out_specs=tok(D_MODEL),
        out_shape=jax.ShapeDtypeStruct(xT.shape, F32),
        compiler_params=pltpu.CompilerParams(dimension_semantics=("arbitrary",), vmem_limit_bytes=VMEM_LIMIT),
        name=f"merge_{l}",
    )(xT, *branches, gb, gb, wgT, wbT, woutT)


def _rope_tables(pos):
    def cs(p, dim):
        inv = 1.0 / (ROPE_THETA ** (jnp.arange(0, dim, 2, dtype=F32) / dim))
        ang = p[:, None] * inv[None, :]
        ang = jnp.concatenate([ang, ang], axis=-1)
        sign = jnp.concatenate([-jnp.ones((dim // 2,), F32), jnp.ones((dim // 2,), F32)])
        return jnp.cos(ang).T, (jnp.sin(ang) * sign[None, :]).T
    row = jnp.floor(pos / GRID_W)
    col = pos - row * GRID_W
    cr, sr = cs(row, HEAD_DIM // 2)
    cc, sc = cs(col, HEAD_DIM // 2)
    c64, s64 = cs(pos, HEAD_DIM)
    c32, s32 = cs(pos, C_ROPE)
    return (jnp.concatenate([cr, cc], axis=0), jnp.concatenate([sr, sc], axis=0), c64, s64, c32, s32)


def kernel(x_prompt, x_sample, norm_g, w_in, a_qk_norm, b_sink, c_q_norm, c_kv_norm, c_w_uq, c_w_ukv,
           d_lambda, d_subln, w_branch, w_out, ffn_wi, ffn_wo):
    b_p, s_p, _ = x_prompt.shape
    b_s, s_s, _ = x_sample.shape
    t_p, t_s = b_p * s_p, b_s * s_s
    t = t_p + t_s
    depth = norm_g.shape[0]
    seq_unit = max(TM, TQC * N_CHAINS, 2 * PIPE_G * TK)
    assert s_p % seq_unit == 0 and s_s % seq_unit == 0
    seq = (t_p, s_p, s_s)

    xT = jnp.concatenate([x_prompt.reshape(t_p, D_MODEL), x_sample.reshape(t_s, D_MODEL)], axis=0).T

    pos = jnp.concatenate([jnp.tile(jnp.arange(s_p, dtype=F32), b_p), jnp.tile(jnp.arange(s_s, dtype=F32), b_s)])
    tabs = _rope_tables(pos)

    lane_bcast = lambda g: jnp.broadcast_to(g[..., None].astype(F32), g.shape + (LANES,))
    gb = lane_bcast(norm_g)
    aqkg = lane_bcast(a_qk_norm)
    cqg = lane_bcast(c_q_norm)
    ckvg = lane_bcast(c_kv_norm)
    dsg = lane_bcast(d_subln)
    tr = lambda w: jnp.swapaxes(w, -1, -2).astype(BF16)
    wiT, woT = tr(ffn_wi), tr(ffn_wo)
    winT, wgT = tr(w_in[:, :, :N_MIX_COLS]), tr(w_in[:, :, N_MIX_COLS:])
    wbT, woutT = tr(w_branch), tr(w_out)
    wuqT, wukvT = tr(c_w_uq), tr(c_w_ukv)
    sink = b_sink.astype(F32)
    dlam = d_lambda.astype(F32)

    nb = t // TK
    for l in range(depth):
        lambda_init = 0.8 - 0.6 * math.exp(-0.3 * l)
        xT = _ffn_call(xT, gb, wiT, woT, l, 0)
        (qa, ka, va, qb, kb, vb, qc, kc, vc, qd, kd, vd) = _prep_call(
            xT, gb, winT, aqkg, cqg, ckvg, wuqT, wukvT, tabs, l)
        k3 = lambda k: k.reshape(nb, TK, k.shape[1])
        oa = _dense_attn_call(qa, k3(ka), va, seq, n_steps=2, n_rows=2, k_block=lambda h: 0, name=f"attn_a_{l}")
        ob = _band_attn_call(sink[l], qb, k3(kb), vb, seq, l)
        oc = _dense_attn_call(qc, k3(kc), vc, seq, n_steps=4, n_rows=1, k_block=lambda h: h, name=f"attn_c_{l}")
        od = _dense_attn_call(qd, k3(kd), vd, seq, n_steps=4, n_rows=2, k_block=lambda h: h // 2,
                              name=f"attn_d_{l}", combine=True, extra=(dlam, dsg),
                              out_scale=1.0 - lambda_init, l=l)
        xT = _merge_call(xT, (oa, ob, oc, od), gb, wgT, wbT, woutT, l)
        xT = _ffn_call(xT, gb, wiT, woT, l, 1)

    y = xT.T
    return (y[:t_p].reshape(b_p, s_p, D_MODEL), y[t_p:].reshape(b_s, s_s, D_MODEL))
```

```python
import functools
import math

import numpy as np
import jax
import jax.numpy as jnp
from jax import lax
from jax.experimental import pallas as pl
from jax.experimental.pallas import tpu as pltpu

D_MODEL = 1024
GRID_W = 64
HEAD_DIM = 64
WINDOW = 128
C_NOPE = 64
C_ROPE = 32
C_Q_LORA = 384
C_KV_LORA = 256
D_HEAD_DIM = 32
N_BRANCH = 4
BRANCH_W = 256
D_FF = 2816
ROPE_THETA = 10000.0
EPS = 1e-6
NEG_BIG = -1e30
LOG2E = 1.4426950408889634
N_MIX_COLS = 2464

LANES = 128
TM = 512
TK = 256
TQC = 256
N_CHAINS = 4
PIPE_G = 1
DV_PAD = 80
FFN_CHUNK = 1408
VMEM_LIMIT = 56 * 1024 * 1024

BF16 = jnp.bfloat16
F32 = jnp.float32


def _dot(a, b):
    return jnp.dot(a, b, preferred_element_type=F32)


def _lanes(gb, n):
    return jnp.tile(gb, (1, n // LANES))


def _rms_rows(x):
    return x * lax.rsqrt(jnp.mean(x * x, axis=0, keepdims=True) + EPS)


def _sigmoid(x):
    return 1.0 / (1.0 + jnp.exp(-x))


def _swap_halves(x, group):
    h = group // 2
    parts = []
    for r in range(0, x.shape[0], group):
        parts += [x[r + h:r + group], x[r:r + h]]
    return jnp.concatenate(parts, axis=0)


def _rope_rows(x, cos, sin_signed, group):
    n = x.shape[0] // cos.shape[0]
    if n > 1:
        cos = jnp.concatenate([cos] * n, axis=0)
        sin_signed = jnp.concatenate([sin_signed] * n, axis=0)
    return x * cos + _swap_halves(x, group) * sin_signed


def _seq_bounds(q0, t_p, s_p, s_s):
    lo_p = (q0 // s_p) * s_p
    lo_s = t_p + ((q0 - t_p) // s_s) * s_s
    in_p = q0 < t_p
    lo = jnp.where(in_p, lo_p, lo_s)
    hi = jnp.where(in_p, lo_p + s_p, lo_s + s_s)
    return lo, hi


def _ffn_kernel(x_ref, gpre_ref, gpost_ref, wi_ref, wo_ref, o_ref):
    x = x_ref[...]
    tm = x.shape[1]
    xn = (_rms_rows(x) * _lanes(gpre_ref[...], tm)).astype(BF16)
    acc = jnp.zeros((D_MODEL, tm), F32)
    for c in range(D_FF // FFN_CHUNK):
        lo = c * FFN_CHUNK
        hg = _dot(wi_ref[lo:lo + FFN_CHUNK, :], xn)
        hu = _dot(wi_ref[D_FF + lo:D_FF + lo + FFN_CHUNK, :], xn)
        a = (hg * _sigmoid(hg) * hu).astype(BF16)
        acc = acc + _dot(wo_ref[:, lo:lo + FFN_CHUNK], a)
    o_ref[...] = x + 0.5 * (_rms_rows(acc) * _lanes(gpost_ref[...], tm))


def _ffn_call(xT, gb, wiT, woT, l, j):
    t = xT.shape[1]
    const = lambda *idx: (lambda i: idx)
    return pl.pallas_call(
        _ffn_kernel,
        grid=(t // TM,),
        in_specs=[
            pl.BlockSpec((D_MODEL, TM), lambda i: (0, i)),
            pl.BlockSpec((None, None, D_MODEL, LANES), const(l, 4 * j, 0, 0)),
            pl.BlockSpec((None, None, D_MODEL, LANES), const(l, 4 * j + 1, 0, 0)),
            pl.BlockSpec((None, None, 2 * D_FF, D_MODEL), const(l, j, 0, 0), pipeline_mode=pl.Buffered(1)),
            pl.BlockSpec((None, None, D_MODEL, D_FF), const(l, j, 0, 0), pipeline_mode=pl.Buffered(1)),
        ],
        out_specs=pl.BlockSpec((D_MODEL, TM), lambda i: (0, i)),
        out_shape=jax.ShapeDtypeStruct(xT.shape, F32),
        compiler_params=pltpu.CompilerParams(dimension_semantics=("arbitrary",), vmem_limit_bytes=VMEM_LIMIT),
        name=f"ffn_{l}_{j}",
    )(xT, gb, gb, wiT, woT)


def _prep_kernel(x_ref, g_ref, win_ref, aqg_ref, akg_ref, cqg_ref, ckvg_ref, wuq_ref, wukv_ref,
                 cos_a_ref, sin_a_ref, cos64_ref, sin64_ref, cos32_ref, sin32_ref,
                 qa_ref, ka_ref, va_ref, qb_ref, kb_ref, vb_ref,
                 qc_ref, kc_ref, vc_ref, qd_ref, kd_ref, vd_ref):
    x = x_ref[...]
    tm = x.shape[1]
    xn = (_rms_rows(x) * _lanes(g_ref[...], tm)).astype(BF16)
    proj = _dot(win_ref[...], xn)
    aq, ak, av = proj[0:256], proj[256:384], proj[384:512]
    bq, bk, bv = proj[512:768], proj[768:896], proj[896:1024]
    cq, ckv, ckpe = proj[1024:1408], proj[1408:1664], proj[1664:1696]
    dq, dk, dv = proj[1696:1952], proj[1952:2208], proj[2208:2464]
    zeros32 = jnp.zeros((32, tm), F32)
    zeros64 = jnp.zeros((64, tm), F32)

    def store_vt(ref, vt):
        vt = vt.astype(BF16)
        for jj in range(tm // TK):
            ref[jj] = vt[:, jj * TK:(jj + 1) * TK]

    ones_rows = jnp.where(lax.broadcasted_iota(jnp.int32, (DV_PAD - HEAD_DIM, tm), 0) == 0, 1.0, 0.0)

    def with_ones(vt):
        parts = []
        for r in range(0, vt.shape[0], HEAD_DIM):
            parts += [vt[r:r + HEAD_DIM], ones_rows]
        return jnp.concatenate(parts, axis=0)

    def pad_kv(y, kv):
        return [y, zeros64] if kv == 0 else [zeros64, y]

    cos_a, sin_a = cos_a_ref[...], sin_a_ref[...]
    aqg, akg = _lanes(aqg_ref[...], tm), _lanes(akg_ref[...], tm)
    sc64 = HEAD_DIM ** -0.5 * LOG2E
    parts = []
    for h in range(4):
        y = _rms_rows(aq[64 * h:64 * h + 64]) * aqg
        y = _rope_rows(y, cos_a, sin_a, 32) * sc64
        parts += pad_kv(y, h // 2)
    qa_ref[...] = jnp.concatenate(parts, axis=0).astype(BF16)
    parts = []
    for kv in range(2):
        y = _rms_rows(ak[64 * kv:64 * kv + 64]) * akg
        parts.append(_rope_rows(y, cos_a, sin_a, 32))
    ka_ref[...] = jnp.concatenate(parts, axis=0).T.astype(BF16)
    store_vt(va_ref, with_ones(av))

    cos64, sin64 = cos64_ref[...], sin64_ref[...]
    yb = _rope_rows(bq, cos64, sin64, 64) * sc64
    parts = []
    for h in range(4):
        parts += pad_kv(yb[64 * h:64 * h + 64], h // 2)
    qb_ref[...] = jnp.concatenate(parts, axis=0).astype(BF16)
    kb_ref[...] = _rope_rows(bk, cos64, sin64, 64).T.astype(BF16)
    store_vt(vb_ref, bv)

    cos32, sin32 = cos32_ref[...], sin32_ref[...]
    cqn = (_rms_rows(cq) * _lanes(cqg_ref[...], tm)).astype(BF16)
    cqh = _dot(wuq_ref[...], cqn)
    kvn = (_rms_rows(ckv) * _lanes(ckvg_ref[...], tm)).astype(BF16)
    kvh = _dot(wukv_ref[...], kvn)
    kpe = _rope_rows(ckpe, cos32, sin32, 32)
    sc96 = (C_NOPE + C_ROPE) ** -0.5 * LOG2E
    qparts, kparts, vparts = [], [], []
    for h in range(4):
        qn = cqh[96 * h:96 * h + 64]
        qp = _rope_rows(cqh[96 * h + 64:96 * h + 96], cos32, sin32, 32)
        qparts += [qn * sc96, qp * sc96, zeros32]
        kparts += [kvh[128 * h:128 * h + 64], kpe, zeros32]
        vparts.append(kvh[128 * h + 64:128 * h + 128])
    qc_ref[...] = jnp.concatenate(qparts, axis=0).astype(BF16)
    kc_ref[...] = jnp.concatenate(kparts, axis=0).T.astype(BF16)
    store_vt(vc_ref, with_ones(jnp.concatenate(vparts, axis=0)))

    sc32 = D_HEAD_DIM ** -0.5 * LOG2E
    yd = _rope_rows(dq, cos32, sin32, 32) * sc32
    parts = []
    for m in range(8):
        blk = [zeros32] * 4
        blk[m % 4] = yd[32 * m:32 * m + 32]
        parts += blk
    qd_ref[...] = jnp.concatenate(parts, axis=0).astype(BF16)
    kd_ref[...] = _rope_rows(dk, cos32, sin32, 32).T.astype(BF16)
    store_vt(vd_ref, with_ones(dv))


def _prep_call(xT, gb, winT, aqkg, cqg, ckvg, wuqT, wukvT, tabs, l):
    t = xT.shape[1]
    nb = t // TK
    const = lambda *idx: (lambda i: idx)
    tok = lambda rows: pl.BlockSpec((rows, TM), lambda i: (0, i))
    nat = lambda cols: pl.BlockSpec((TM, cols), lambda i: (i, 0))
    vts = lambda rows: pl.BlockSpec((TM // TK, rows, TK), lambda i: (i, 0, 0))
    sds = jax.ShapeDtypeStruct
    out_shape = [
        sds((512, t), BF16), sds((t, 128), BF16), sds((nb, 2 * DV_PAD, TK), BF16),
        sds((512, t), BF16), sds((t, 128), BF16), sds((nb, 128, TK), BF16),
        sds((512, t), BF16), sds((t, 512), BF16), sds((nb, 4 * DV_PAD, TK), BF16),
        sds((1024, t), BF16), sds((t, 256), BF16), sds((nb, 4 * DV_PAD, TK), BF16),
    ]
    out_specs = [tok(512), nat(128), vts(2 * DV_PAD), tok(512), nat(128), vts(128),
                 tok(512), nat(512), vts(4 * DV_PAD), tok(1024), nat(256), vts(4 * DV_PAD)]
    in_specs = [
        tok(D_MODEL),
        pl.BlockSpec((None, None, D_MODEL, LANES), const(l, 2, 0, 0)),
        pl.BlockSpec((None, N_MIX_COLS, D_MODEL), const(l, 0, 0), pipeline_mode=pl.Buffered(1)),
        pl.BlockSpec((None, None, HEAD_DIM, LANES), const(l, 0, 0, 0)),
        pl.BlockSpec((None, None, HEAD_DIM, LANES), const(l, 1, 0, 0)),
        pl.BlockSpec((None, C_Q_LORA, LANES), const(l, 0, 0)),
        pl.BlockSpec((None, C_KV_LORA, LANES), const(l, 0, 0)),
        pl.BlockSpec((None, 384, C_Q_LORA), const(l, 0, 0)),
        pl.BlockSpec((None, 512, C_KV_LORA), const(l, 0, 0)),
        tok(64), tok(64), tok(64), tok(64), tok(32), tok(32),
    ]
    return pl.pallas_call(
        _prep_kernel,
        grid=(t // TM,),
        in_specs=in_specs,
        out_specs=out_specs,
        out_shape=out_shape,
        compiler_params=pltpu.CompilerParams(dimension_semantics=("arbitrary",), vmem_limit_bytes=VMEM_LIMIT),
        name=f"prep_{l}",
    )(xT, gb, winT, aqkg, aqkg, cqg, ckvg, wuqT, wukvT, *tabs)


def _dense_attn_kernel(*refs, n_rows, combine, seq, out_scale):
    if combine:
        q_ref, k_ref, v_ref, lam_ref, g_ref, o_ref, s_ref, p_ref, al_ref, m_ref, acc_ref = refs
    else:
        q_ref, k_ref, v_ref, o_ref, s_ref, p_ref, al_ref, m_ref, acc_ref = refs
    tq = q_ref.shape[1]
    n_cg = N_CHAINS // n_rows
    lo, hi = _seq_bounds(pl.program_id(1) * tq, *seq)
    lo_b = lo // TK
    hi_b = hi // TK
    chains = [(r, cg) for r in range(n_rows) for cg in range(n_cg)]
    g2 = PIPE_G

    def q_of(c):
        r, cg = chains[c]
        return q_ref[128 * r:128 * r + 128, TQC * cg:TQC * cg + TQC]

    def mxu_stage(kb_pv, half, kb_s, slot, do_pv=True, do_s=True):
        for g in range(g2):
            if do_pv:
                vblk = v_ref[kb_pv + g]
            if do_s:
                kblk = k_ref[kb_s + g]
            for c in range(N_CHAINS):
                if do_s:
                    s_ref[slot, g, c] = _dot(kblk, q_of(c))
                if do_pv:
                    acc_ref[c] = al_ref[half, g, c] * acc_ref[c] + _dot(vblk, p_ref[half, g, c])

    def softmax_stage(slot, half):
        for g in range(g2):
            for c in range(N_CHAINS):
                s = s_ref[slot, g, c]
                m = m_ref[c]
                m_new = jnp.maximum(m, jnp.max(s, axis=0, keepdims=True))
                al_ref[half, g, c] = jnp.exp2(m - m_new)
                p_ref[half, g, c] = jnp.exp2(s - m_new).astype(BF16)
                m_ref[c] = m_new

    def trip(kb, first, last):
        mxu_stage(kb - 2 * g2, 0, kb + g2, 1, do_pv=not first)
        softmax_stage(0, 0)
        mxu_stage(kb - g2, 1, kb + 2 * g2, 0, do_pv=not first, do_s=not last)
        softmax_stage(1, 1)

    for c in range(N_CHAINS):
        m_ref[c] = jnp.full((1, TQC), NEG_BIG, F32)
        acc_ref[c] = jnp.zeros((DV_PAD, TQC), F32)
    mxu_stage(0, 0, lo_b, 0, do_pv=False)

    n_trips = (hi_b - lo_b) // (2 * g2)

    def single_trip():
        trip(lo_b, True, True)

    def multi_trip():
        trip(lo_b, True, False)

        def body(it, carry):
            trip(lo_b + 2 * g2 * it, False, False)
            return carry

        lax.fori_loop(1, n_trips - 1, body, 0)
        trip(hi_b - 2 * g2, False, True)

    lax.cond(n_trips == 1, single_trip, multi_trip)
    mxu_stage(hi_b - 2 * g2, 0, 0, 0, do_s=False)
    mxu_stage(hi_b - g2, 1, 0, 0, do_s=False)

    outs = []
    for c in range(N_CHAINS):
        acc = acc_ref[c]
        outs.append(acc[0:HEAD_DIM] / acc[HEAD_DIM:HEAD_DIM + 1])
    if not combine:
        for c, (r, cg) in enumerate(chains):
            o_ref[HEAD_DIM * r:HEAD_DIM * (r + 1), TQC * cg:TQC * (cg + 1)] = outs[c].astype(o_ref.dtype)
    else:
        lf = lam_ref[...]
        lam = (jnp.exp(jnp.sum(lf[0:1] * lf[1:2], axis=1, keepdims=True))
               - jnp.exp(jnp.sum(lf[2:3] * lf[3:4], axis=1, keepdims=True)) + (1.0 - out_scale))
        gain = _lanes(g_ref[...], TQC)
        for cg in range(n_cg):
            o = outs[cg] - lam * outs[n_cg + cg]
            o_ref[:, TQC * cg:TQC * (cg + 1)] = (_rms_rows(o) * gain * out_scale).astype(o_ref.dtype)


def _dense_attn_call(qT, k3, v3, seq, *, n_steps, n_rows, k_block, name, combine=False, extra=(), out_scale=1.0,
                     l=0):
    t = qT.shape[1]
    nb = k3.shape[0]
    tq = TQC * (N_CHAINS // n_rows)
    out_rows = HEAD_DIM if combine else HEAD_DIM * n_rows
    in_specs = [
        pl.BlockSpec((128 * n_rows, tq), lambda h, i: (h, i)),
        pl.BlockSpec((nb, TK, 128), lambda h, i: (0, 0, k_block(h))),
        pl.BlockSpec((nb, DV_PAD, TK), lambda h, i: (0, h, 0)),
    ]
    if combine:
        in_specs += [pl.BlockSpec((None, 4, D_HEAD_DIM), lambda h, i: (l, 0, 0)),
                     pl.BlockSpec((None, 2 * D_HEAD_DIM, LANES), lambda h, i: (l, 0, 0))]
    tiles = (2, PIPE_G, N_CHAINS)
    return pl.pallas_call(
        functools.partial(_dense_attn_kernel, n_rows=n_rows, combine=combine, seq=seq, out_scale=out_scale),
        grid=(n_steps, t // tq),
        in_specs=in_specs,
        out_specs=pl.BlockSpec((out_rows, tq), lambda h, i: (h, i)),
        out_shape=jax.ShapeDtypeStruct((n_steps * out_rows, t), BF16),
        scratch_shapes=[pltpu.VMEM(tiles + (TK, TQC), F32), pltpu.VMEM(tiles + (TK, TQC), BF16),
                        pltpu.VMEM(tiles + (1, TQC), F32), pltpu.VMEM((N_CHAINS, 1, TQC), F32),
                        pltpu.VMEM((N_CHAINS, DV_PAD, TQC), F32)],
        compiler_params=pltpu.CompilerParams(dimension_semantics=("arbitrary", "arbitrary"),
                                             vmem_limit_bytes=VMEM_LIMIT),
        name=name,
    )(qT, k3, v3, *extra)


def _band_attn_kernel(sink_ref, q_ref, k0_ref, k1_ref, k2_ref, v0_ref, v1_ref, v2_ref, o_ref, *, seq):
    i = pl.program_id(0)
    tq = q_ref.shape[1]
    lo, hi = _seq_bounds(i * tq, *seq)
    spans = ((TK - WINDOW, TK), (0, TK), (0, WINDOW))
    k_refs = (k0_ref, k1_ref, k2_ref)
    v_refs = (v0_ref, v1_ref, v2_ref)
    valid = []
    for j, (r0, r1) in enumerate(spans):
        shape = (r1 - r0, tq)
        qpos = i * tq + lax.broadcasted_iota(jnp.int32, shape, 1)
        kpos = (i - 1 + j) * TK + r0 + lax.broadcasted_iota(jnp.int32, shape, 0)
        d = kpos - qpos
        ok = jnp.where(d <= WINDOW, jnp.where(d >= -WINDOW, 1, 0), 0)
        ok = jnp.where(kpos >= lo, jnp.where(kpos < hi, ok, 0), 0)
        valid.append(ok > 0)
    scores = [[jnp.where(valid[j], _dot(k_refs[j][0][r0:r1, :], q_ref[128 * h:128 * h + 128, :]), NEG_BIG)
               for j, (r0, r1) in enumerate(spans)] for h in range(4)]
    for h in range(4):
        kv = h // 2
        s = scores[h]
        sink = sink_ref[h] * LOG2E
        m = jnp.maximum(jnp.maximum(jnp.max(s[0], axis=0, keepdims=True), jnp.max(s[1], axis=0, keepdims=True)),
                        jnp.maximum(jnp.max(s[2], axis=0, keepdims=True), sink))
        l = jnp.exp2(sink - m)
        acc = jnp.zeros((HEAD_DIM, tq), F32)
        for j, (r0, r1) in enumerate(spans):
            p = jnp.exp2(s[j] - m)
            l = l + jnp.sum(p, axis=0, keepdims=True)
            acc = acc + _dot(v_refs[j][0][64 * kv:64 * kv + 64, r0:r1], p.astype(BF16))
        o_ref[64 * h:64 * h + 64, :] = (acc / l).astype(o_ref.dtype)


def _band_attn_call(sink, qT, k3, v3, seq, l):
    t = qT.shape[1]
    nb = t // TK
    clamp = lambda b: jnp.clip(b, 0, nb - 1)
    kspec = lambda off: pl.BlockSpec((1, TK, 128), lambda i: (clamp(i + off), 0, 0))
    vspec = lambda off: pl.BlockSpec((1, 128, TK), lambda i: (clamp(i + off), 0, 0))
    return pl.pallas_call(
        functools.partial(_band_attn_kernel, seq=seq),
        grid=(nb,),
        in_specs=[pl.BlockSpec(memory_space=pltpu.SMEM),
                  pl.BlockSpec((512, TK), lambda i: (0, i)),
                  kspec(-1), kspec(0), kspec(1), vspec(-1), vspec(0), vspec(1)],
        out_specs=pl.BlockSpec((256, TK), lambda i: (0, i)),
        out_shape=jax.ShapeDtypeStruct((256, t), BF16),
        compiler_params=pltpu.CompilerParams(dimension_semantics=("arbitrary",), vmem_limit_bytes=VMEM_LIMIT),
        name=f"attn_b_{l}",
    )(sink, qT, k3, k3, k3, v3, v3, v3)


def _merge_kernel(x_ref, oa_ref, ob_ref, oc_ref, od_ref, gpre_ref, gpost_ref, wg_ref, wb_ref, wout_ref, o_ref):
    x = x_ref[...]
    tm = x.shape[1]
    xn = (_rms_rows(x) * _lanes(gpre_ref[...], tm)).astype(BF16)
    merged = jnp.zeros((D_MODEL, tm), F32)
    for n, br_ref in enumerate((oa_ref, ob_ref, oc_ref, od_ref)):
        gl = _dot(wg_ref[n * D_MODEL:(n + 1) * D_MODEL, :], xn)
        bp = _dot(wb_ref[n], br_ref[...])
        merged = merged + _sigmoid(gl) * bp
    mix = _dot(wout_ref[...], merged.astype(BF16))
    o_ref[...] = x + _rms_rows(mix) * _lanes(gpost_ref[...], tm)


def _merge_call(xT, branches, gb, wgT, wbT, woutT, l):
    t = xT.shape[1]
    const = lambda *idx: (lambda i: idx)
    tok = lambda rows: pl.BlockSpec((rows, TM), lambda i: (0, i))
    return pl.pallas_call(
        _merge_kernel,
        grid=(t // TM,),
        in_specs=[
            tok(D_MODEL), tok(BRANCH_W), tok(BRANCH_W), tok(BRANCH_W), tok(BRANCH_W),
            pl.BlockSpec((None, None, D_MODEL, LANES), const(l, 2, 0, 0)),
            pl.BlockSpec((None, None, D_MODEL, LANES), const(l, 3, 0, 0)),
            pl.BlockSpec((None, N_BRANCH * D_MODEL, D_MODEL), const(l, 0, 0), pipeline_mode=pl.Buffered(1)),
            pl.BlockSpec((None, N_BRANCH, D_MODEL, BRANCH_W), const(l, 0, 0, 0), pipeline_mode=pl.Buffered(1)),
            pl.BlockSpec((None, D_MODEL, D_MODEL), const(l, 0, 0), pipeline_mode=pl.Buffered(1)),
        ],
        out_specs=tok(D_MODEL),
        out_shape=jax.ShapeDtypeStruct(xT.shape, F32),
        compiler_params=pltpu.CompilerParams(dimension_semantics=("arbitrary",), vmem_limit_bytes=VMEM_LIMIT),
        name=f"merge_{l}",
    )(xT, *branches, gb, gb, wgT, wbT, woutT)


def _rope_tables(pos):
    def cs(p, dim):
        inv = 1.0 / (ROPE_THETA ** (jnp.arange(0, dim, 2, dtype=F32) / dim))
        ang = p[:, None] * inv[None, :]
        ang = jnp.concatenate([ang, ang], axis=-1)
        sign = jnp.concatenate([-jnp.ones((dim // 2,), F32), jnp.ones((dim // 2,), F32)])
        return jnp.cos(ang).T, (jnp.sin(ang) * sign[None, :]).T
    row = jnp.floor(pos / GRID_W)
    col = pos - row * GRID_W
    cr, sr = cs(row, HEAD_DIM // 2)
    cc, sc = cs(col, HEAD_DIM // 2)
    c64, s64 = cs(pos, HEAD_DIM)
    c32, s32 = cs(pos, C_ROPE)
    return (jnp.concatenate([cr, cc], axis=0), jnp.concatenate([sr, sc], axis=0), c64, s64, c32, s32)


def kernel(x_prompt, x_sample, norm_g, w_in, a_qk_norm, b_sink, c_q_norm, c_kv_norm, c_w_uq, c_w_ukv,
           d_lambda, d_subln, w_branch, w_out, ffn_wi, ffn_wo):
    b_p, s_p, _ = x_prompt.shape
    b_s, s_s, _ = x_sample.shape
    t_p, t_s = b_p * s_p, b_s * s_s
    t = t_p + t_s
    depth = norm_g.shape[0]
    seq_unit = max(TM, TQC * N_CHAINS, 2 * PIPE_G * TK)
    assert s_p % seq_unit == 0 and s_s % seq_unit == 0
    seq = (t_p, s_p, s_s)

    xT = jnp.concatenate([x_prompt.reshape(t_p, D_MODEL), x_sample.reshape(t_s, D_MODEL)], axis=0).T

    pos = jnp.concatenate([jnp.tile(jnp.arange(s_p, dtype=F32), b_p), jnp.tile(jnp.arange(s_s, dtype=F32), b_s)])
    tabs = _rope_tables(pos)

    lane_bcast = lambda g: jnp.broadcast_to(g[..., None].astype(F32), g.shape + (LANES,))
    gb = lane_bcast(norm_g)
    aqkg = lane_bcast(a_qk_norm)
    cqg = lane_bcast(c_q_norm)
    ckvg = lane_bcast(c_kv_norm)
    dsg = lane_bcast(d_subln)
    tr = lambda w: jnp.swapaxes(w, -1, -2).astype(BF16)
    wiT, woT = tr(ffn_wi), tr(ffn_wo)
    winT, wgT = tr(w_in[:, :, :N_MIX_COLS]), tr(w_in[:, :, N_MIX_COLS:])
    wbT, woutT = tr(w_branch), tr(w_out)
    wuqT, wukvT = tr(c_w_uq), tr(c_w_ukv)
    sink = b_sink.astype(F32)
    dlam = d_lambda.astype(F32)

    nb = t // TK
    for l in range(depth):
        lambda_init = 0.8 - 0.6 * math.exp(-0.3 * l)
        xT = _ffn_call(xT, gb, wiT, woT, l, 0)
        (qa, ka, va, qb, kb, vb, qc, kc, vc, qd, kd, vd) = _prep_call(
            xT, gb, winT, aqkg, cqg, ckvg, wuqT, wukvT, tabs, l)
        k3 = lambda k: k.reshape(nb, TK, k.shape[1])
        oa = _dense_attn_call(qa, k3(ka), va, seq, n_steps=2, n_rows=2, k_block=lambda h: 0, name=f"attn_a_{l}")
        ob = _band_attn_call(sink[l], qb, k3(kb), vb, seq, l)
        oc = _dense_attn_call(qc, k3(kc), vc, seq, n_steps=4, n_rows=1, k_block=lambda h: h, name=f"attn_c_{l}")
        od = _dense_attn_call(qd, k3(kd), vd, seq, n_steps=4, n_rows=2, k_block=lambda h: h // 2,
                              name=f"attn_d_{l}", combine=True, extra=(dlam, dsg),
                              out_scale=1.0 - lambda_init, l=l)
        xT = _merge_call(xT, (oa, ob, oc, od), gb, wgT, wbT, woutT, l)
        xT = _ffn_call(xT, gb, wiT, woT, l, 1)

    y = xT.T
    return (y[:t_p].reshape(b_p, s_p, D_MODEL), y[t_p:].reshape(b_s, s_s, D_MODEL))
```

```python
import functools
import math

import numpy as np
import jax
import jax.numpy as jnp
from jax import lax
from jax.experimental import pallas as pl
from jax.experimental.pallas import tpu as pltpu

D_MODEL = 1024
GRID_W = 64
HEAD_DIM = 64
WINDOW = 128
C_NOPE = 64
C_ROPE = 32
C_Q_LORA = 384
C_KV_LORA = 256
D_HEAD_DIM = 32
N_BRANCH = 4
BRANCH_W = 256
D_FF = 2816
ROPE_THETA = 10000.0
EPS = 1e-6
NEG_BIG = -1e30
LOG2E = 1.4426950408889634
N_MIX_COLS = 2464

LANES = 128
TM = 512
TK = 256
TQC = 256
N_CHAINS = 4
PIPE_G = 1
PIPE_HS = 4
DV_PAD = 80
FFN_CHUNK = 1408
VMEM_LIMIT = 56 * 1024 * 1024

BF16 = jnp.bfloat16
F32 = jnp.float32


def _dot(a, b):
    return jnp.dot(a, b, preferred_element_type=F32)


def _lanes(gb, n):
    return jnp.tile(gb, (1, n // LANES))


def _rms_rows(x):
    return x * lax.rsqrt(jnp.mean(x * x, axis=0, keepdims=True) + EPS)


def _sigmoid(x):
    return 1.0 / (1.0 + jnp.exp(-x))


def _swap_halves(x, group):
    h = group // 2
    parts = []
    for r in range(0, x.shape[0], group):
        parts += [x[r + h:r + group], x[r:r + h]]
    return jnp.concatenate(parts, axis=0)


def _rope_rows(x, cos, sin_signed, group):
    n = x.shape[0] // cos.shape[0]
    if n > 1:
        cos = jnp.concatenate([cos] * n, axis=0)
        sin_signed = jnp.concatenate([sin_signed] * n, axis=0)
    return x * cos + _swap_halves(x, group) * sin_signed


def _seq_bounds(q0, t_p, s_p, s_s):
    lo_p = (q0 // s_p) * s_p
    lo_s = t_p + ((q0 - t_p) // s_s) * s_s
    in_p = q0 < t_p
    lo = jnp.where(in_p, lo_p, lo_s)
    hi = jnp.where(in_p, lo_p + s_p, lo_s + s_s)
    return lo, hi


def _ffn_kernel(x_ref, gpre_ref, gpost_ref, wi_ref, wo_ref, o_ref):
    x = x_ref[...]
    tm = x.shape[1]
    xn = (_rms_rows(x) * _lanes(gpre_ref[...], tm)).astype(BF16)
    acc = jnp.zeros((D_MODEL, tm), F32)
    for c in range(D_FF // FFN_CHUNK):
        lo = c * FFN_CHUNK
        hg = _dot(wi_ref[lo:lo + FFN_CHUNK, :], xn)
        hu = _dot(wi_ref[D_FF + lo:D_FF + lo + FFN_CHUNK, :], xn)
        a = (hg * _sigmoid(hg) * hu).astype(BF16)
        acc = acc + _dot(wo_ref[:, lo:lo + FFN_CHUNK], a)
    o_ref[...] = x + 0.5 * (_rms_rows(acc) * _lanes(gpost_ref[...], tm))


def _ffn_call(xT, gb, wiT, woT, l, j):
    t = xT.shape[1]
    const = lambda *idx: (lambda i: idx)
    return pl.pallas_call(
        _ffn_kernel,
        grid=(t // TM,),
        in_specs=[
            pl.BlockSpec((D_MODEL, TM), lambda i: (0, i)),
            pl.BlockSpec((None, None, D_MODEL, LANES), const(l, 4 * j, 0, 0)),
            pl.BlockSpec((None, None, D_MODEL, LANES), const(l, 4 * j + 1, 0, 0)),
            pl.BlockSpec((None, None, 2 * D_FF, D_MODEL), const(l, j, 0, 0), pipeline_mode=pl.Buffered(1)),
            pl.BlockSpec((None, None, D_MODEL, D_FF), const(l, j, 0, 0), pipeline_mode=pl.Buffered(1)),
        ],
        out_specs=pl.BlockSpec((D_MODEL, TM), lambda i: (0, i)),
        out_shape=jax.ShapeDtypeStruct(xT.shape, F32),
        compiler_params=pltpu.CompilerParams(dimension_semantics=("arbitrary",), vmem_limit_bytes=VMEM_LIMIT),
        name=f"ffn_{l}_{j}",
    )(xT, gb, gb, wiT, woT)


def _prep_kernel(x_ref, g_ref, win_ref, aqg_ref, akg_ref, cqg_ref, ckvg_ref, wuq_ref, wukv_ref,
                 cos_a_ref, sin_a_ref, cos64_ref, sin64_ref, cos32_ref, sin32_ref,
                 qa_ref, ka_ref, va_ref, qb_ref, kb_ref, vb_ref,
                 qc_ref, kc_ref, vc_ref, qd_ref, kd_ref, vd_ref):
    x = x_ref[...]
    tm = x.shape[1]
    xn = (_rms_rows(x) * _lanes(g_ref[...], tm)).astype(BF16)
    proj = _dot(win_ref[...], xn)
    aq, ak, av = proj[0:256], proj[256:384], proj[384:512]
    bq, bk, bv = proj[512:768], proj[768:896], proj[896:1024]
    cq, ckv, ckpe = proj[1024:1408], proj[1408:1664], proj[1664:1696]
    dq, dk, dv = proj[1696:1952], proj[1952:2208], proj[2208:2464]
    zeros32 = jnp.zeros((32, tm), F32)
    zeros64 = jnp.zeros((64, tm), F32)

    def store_vt(ref, vt):
        vt = vt.astype(BF16)
        for jj in range(tm // TK):
            ref[jj] = vt[:, jj * TK:(jj + 1) * TK]

    ones_rows = jnp.where(lax.broadcasted_iota(jnp.int32, (DV_PAD - HEAD_DIM, tm), 0) == 0, 1.0, 0.0)

    def with_ones(vt):
        parts = []
        for r in range(0, vt.shape[0], HEAD_DIM):
            parts += [vt[r:r + HEAD_DIM], ones_rows]
        return jnp.concatenate(parts, axis=0)

    def pad_kv(y, kv):
        return [y, zeros64] if kv == 0 else [zeros64, y]

    cos_a, sin_a = cos_a_ref[...], sin_a_ref[...]
    aqg, akg = _lanes(aqg_ref[...], tm), _lanes(akg_ref[...], tm)
    sc64 = HEAD_DIM ** -0.5 * LOG2E
    parts = []
    for h in range(4):
        y = _rms_rows(aq[64 * h:64 * h + 64]) * aqg
        y = _rope_rows(y, cos_a, sin_a, 32) * sc64
        parts += pad_kv(y, h // 2)
    qa_ref[...] = jnp.concatenate(parts, axis=0).astype(BF16)
    parts = []
    for kv in range(2):
        y = _rms_rows(ak[64 * kv:64 * kv + 64]) * akg
        parts.append(_rope_rows(y, cos_a, sin_a, 32))
    ka_ref[...] = jnp.concatenate(parts, axis=0).T.astype(BF16)
    store_vt(va_ref, with_ones(av))

    cos64, sin64 = cos64_ref[...], sin64_ref[...]
    yb = _rope_rows(bq, cos64, sin64, 64) * sc64
    parts = []
    for h in range(4):
        parts += pad_kv(yb[64 * h:64 * h + 64], h // 2)
    qb_ref[...] = jnp.concatenate(parts, axis=0).astype(BF16)
    kb_ref[...] = _rope_rows(bk, cos64, sin64, 64).T.astype(BF16)
    store_vt(vb_ref, bv)

    cos32, sin32 = cos32_ref[...], sin32_ref[...]
    cqn = (_rms_rows(cq) * _lanes(cqg_ref[...], tm)).astype(BF16)
    cqh = _dot(wuq_ref[...], cqn)
    kvn = (_rms_rows(ckv) * _lanes(ckvg_ref[...], tm)).astype(BF16)
    kvh = _dot(wukv_ref[...], kvn)
    kpe = _rope_rows(ckpe, cos32, sin32, 32)
    sc96 = (C_NOPE + C_ROPE) ** -0.5 * LOG2E
    qparts, kparts, vparts = [], [], []
    for h in range(4):
        qn = cqh[96 * h:96 * h + 64]
        qp = _rope_rows(cqh[96 * h + 64:96 * h + 96], cos32, sin32, 32)
        qparts += [qn * sc96, qp * sc96, zeros32]
        kparts += [kvh[128 * h:128 * h + 64], kpe, zeros32]
        vparts.append(kvh[128 * h + 64:128 * h + 128])
    qc_ref[...] = jnp.concatenate(qparts, axis=0).astype(BF16)
    kc_ref[...] = jnp.concatenate(kparts, axis=0).T.astype(BF16)
    store_vt(vc_ref, with_ones(jnp.concatenate(vparts, axis=0)))

    sc32 = D_HEAD_DIM ** -0.5 * LOG2E
    yd = _rope_rows(dq, cos32, sin32, 32) * sc32
    parts = []
    for m in range(8):
        blk = [zeros32] * 4
        blk[m % 4] = yd[32 * m:32 * m + 32]
        parts += blk
    qd_ref[...] = jnp.concatenate(parts, axis=0).astype(BF16)
    kd_ref[...] = _rope_rows(dk, cos32, sin32, 32).T.astype(BF16)
    store_vt(vd_ref, with_ones(dv))


def _prep_call(xT, gb, winT, aqkg, cqg, ckvg, wuqT, wukvT, tabs, l):
    t = xT.shape[1]
    nb = t // TK
    const = lambda *idx: (lambda i: idx)
    tok = lambda rows: pl.BlockSpec((rows, TM), lambda i: (0, i))
    nat = lambda cols: pl.BlockSpec((TM, cols), lambda i: (i, 0))
    vts = lambda rows: pl.BlockSpec((TM // TK, rows, TK), lambda i: (i, 0, 0))
    sds = jax.ShapeDtypeStruct
    out_shape = [
        sds((512, t), BF16), sds((t, 128), BF16), sds((nb, 2 * DV_PAD, TK), BF16),
        sds((512, t), BF16), sds((t, 128), BF16), sds((nb, 128, TK), BF16),
        sds((512, t), BF16), sds((t, 512), BF16), sds((nb, 4 * DV_PAD, TK), BF16),
        sds((1024, t), BF16), sds((t, 256), BF16), sds((nb, 4 * DV_PAD, TK), BF16),
    ]
    out_specs = [tok(512), nat(128), vts(2 * DV_PAD), tok(512), nat(128), vts(128),
                 tok(512), nat(512), vts(4 * DV_PAD), tok(1024), nat(256), vts(4 * DV_PAD)]
    in_specs = [
        tok(D_MODEL),
        pl.BlockSpec((None, None, D_MODEL, LANES), const(l, 2, 0, 0)),
        pl.BlockSpec((None, N_MIX_COLS, D_MODEL), const(l, 0, 0), pipeline_mode=pl.Buffered(1)),
        pl.BlockSpec((None, None, HEAD_DIM, LANES), const(l, 0, 0, 0)),
        pl.BlockSpec((None, None, HEAD_DIM, LANES), const(l, 1, 0, 0)),
        pl.BlockSpec((None, C_Q_LORA, LANES), const(l, 0, 0)),
        pl.BlockSpec((None, C_KV_LORA, LANES), const(l, 0, 0)),
        pl.BlockSpec((None, 384, C_Q_LORA), const(l, 0, 0)),
        pl.BlockSpec((None, 512, C_KV_LORA), const(l, 0, 0)),
        tok(64), tok(64), tok(64), tok(64), tok(32), tok(32),
    ]
    return pl.pallas_call(
        _prep_kernel,
        grid=(t // TM,),
        in_specs=in_specs,
        out_specs=out_specs,
        out_shape=out_shape,
        compiler_params=pltpu.CompilerParams(dimension_semantics=("arbitrary",), vmem_limit_bytes=VMEM_LIMIT),
        name=f"prep_{l}",
    )(xT, gb, winT, aqkg, aqkg, cqg, ckvg, wuqT, wukvT, *tabs)


def _dense_attn_kernel(*refs, n_rows, combine, seq, out_scale):
    if combine:
        q_ref, k_ref, v_ref, lam_ref, g_ref, o_ref, s_ref, p_ref, al_ref, m_ref, acc_ref = refs
    else:
        q_ref, k_ref, v_ref, o_ref, s_ref, p_ref, al_ref, m_ref, acc_ref = refs
    tq = q_ref.shape[1]
    n_cg = N_CHAINS // n_rows
    lo, hi = _seq_bounds(pl.program_id(1) * tq, *seq)
    lo_b = lo // TK
    hi_b = hi // TK
    chains = [(r, cg) for r in range(n_rows) for cg in range(n_cg)]
    g2 = PIPE_G
    hs = PIPE_HS

    def q_of(c):
        r, cg = chains[c]
        return q_ref[128 * r:128 * r + 128, TQC * cg:TQC * cg + TQC]

    def mxu_stage(kb_pv, half, kb_s, slot, do_pv=True, do_s=True):
        for g in range(g2):
            if do_pv:
                vblk = v_ref[kb_pv + g]
            if do_s:
                kblk = k_ref[kb_s + g]
            for c in range(N_CHAINS):
                if do_s:
                    s_ref[slot, g, c] = _dot(kblk, q_of(c))
                if do_pv:
                    acc_ref[c] = al_ref[half, g, c] * acc_ref[c] + _dot(vblk, p_ref[half, g, c])

    def softmax_stage(slot, half):
        for g in range(g2):
            for c in range(N_CHAINS):
                s = s_ref[slot, g, c]
                m = m_ref[c]
                m_new = jnp.maximum(m, jnp.max(s, axis=0, keepdims=True))
                al_ref[half, g, c] = jnp.exp2(m - m_new)
                p_ref[half, g, c] = jnp.exp2(s - m_new).astype(BF16)
                m_ref[c] = m_new

    def trip(kb, first, last):
        for j in range(hs):
            mxu_stage(kb + (j - hs) * g2, j, kb + (j + 1) * g2, (j + 1) % hs, do_pv=not first,
                      do_s=not (last and j == hs - 1))
            softmax_stage(j, j)

    for c in range(N_CHAINS):
        m_ref[c] = jnp.full((1, TQC), NEG_BIG, F32)
        acc_ref[c] = jnp.zeros((DV_PAD, TQC), F32)
    mxu_stage(0, 0, lo_b, 0, do_pv=False)

    n_trips = (hi_b - lo_b) // (hs * g2)

    def single_trip():
        trip(lo_b, True, True)

    def multi_trip():
        trip(lo_b, True, False)

        def body(it, carry):
            trip(lo_b + hs * g2 * it, False, False)
            return carry

        lax.fori_loop(1, n_trips - 1, body, 0)
        trip(hi_b - hs * g2, False, True)

    lax.cond(n_trips == 1, single_trip, multi_trip)
    for j in range(hs):
        mxu_stage(hi_b - (hs - j) * g2, j, 0, 0, do_s=False)

    outs = []
    for c in range(N_CHAINS):
        acc = acc_ref[c]
        outs.append(acc[0:HEAD_DIM] / acc[HEAD_DIM:HEAD_DIM + 1])
    if not combine:
        for c, (r, cg) in enumerate(chains):
            o_ref[HEAD_DIM * r:HEAD_DIM * (r + 1), TQC * cg:TQC * (cg + 1)] = outs[c].astype(o_ref.dtype)
    else:
        lf = lam_ref[...]
        lam = (jnp.exp(jnp.sum(lf[0:1] * lf[1:2], axis=1, keepdims=True))
               - jnp.exp(jnp.sum(lf[2:3] * lf[3:4], axis=1, keepdims=True)) + (1.0 - out_scale))
        gain = _lanes(g_ref[...], TQC)
        for cg in range(n_cg):
            o = outs[cg] - lam * outs[n_cg + cg]
            o_ref[:, TQC * cg:TQC * (cg + 1)] = (_rms_rows(o) * gain * out_scale).astype(o_ref.dtype)


def _dense_attn_call(qT, k3, v3, seq, *, n_steps, n_rows, k_block, name, combine=False, extra=(), out_scale=1.0,
                     l=0):
    t = qT.shape[1]
    nb = k3.shape[0]
    tq = TQC * (N_CHAINS // n_rows)
    out_rows = HEAD_DIM if combine else HEAD_DIM * n_rows
    in_specs = [
        pl.BlockSpec((128 * n_rows, tq), lambda h, i: (h, i)),
        pl.BlockSpec((nb, TK, 128), lambda h, i: (0, 0, k_block(h))),
        pl.BlockSpec((nb, DV_PAD, TK), lambda h, i: (0, h, 0)),
    ]
    if combine:
        in_specs += [pl.BlockSpec((None, 4, D_HEAD_DIM), lambda h, i: (l, 0, 0)),
                     pl.BlockSpec((None, 2 * D_HEAD_DIM, LANES), lambda h, i: (l, 0, 0))]
    tiles = (PIPE_HS, PIPE_G, N_CHAINS)
    return pl.pallas_call(
        functools.partial(_dense_attn_kernel, n_rows=n_rows, combine=combine, seq=seq, out_scale=out_scale),
        grid=(n_steps, t // tq),
        in_specs=in_specs,
        out_specs=pl.BlockSpec((out_rows, tq), lambda h, i: (h, i)),
        out_shape=jax.ShapeDtypeStruct((n_steps * out_rows, t), BF16),
        scratch_shapes=[pltpu.VMEM(tiles + (TK, TQC), F32), pltpu.VMEM(tiles + (TK, TQC), BF16),
                        pltpu.VMEM(tiles + (1, TQC), F32), pltpu.VMEM((N_CHAINS, 1, TQC), F32),
                        pltpu.VMEM((N_CHAINS, DV_PAD, TQC), F32)],
        compiler_params=pltpu.CompilerParams(dimension_semantics=("arbitrary", "arbitrary"),
                                             vmem_limit_bytes=VMEM_LIMIT),
        name=name,
    )(qT, k3, v3, *extra)


def _band_attn_kernel(sink_ref, q_ref, k0_ref, k1_ref, k2_ref, v0_ref, v1_ref, v2_ref, o_ref, *, seq):
    i = pl.program_id(0)
    tq = q_ref.shape[1]
    lo, hi = _seq_bounds(i * tq, *seq)
    spans = ((TK - WINDOW, TK), (0, TK), (0, WINDOW))
    k_refs = (k0_ref, k1_ref, k2_ref)
    v_refs = (v0_ref, v1_ref, v2_ref)
    valid = []
    for j, (r0, r1) in enumerate(spans):
        shape = (r1 - r0, tq)
        qpos = i * tq + lax.broadcasted_iota(jnp.int32, shape, 1)
        kpos = (i - 1 + j) * TK + r0 + lax.broadcasted_iota(jnp.int32, shape, 0)
        d = kpos - qpos
        ok = jnp.where(d <= WINDOW, jnp.where(d >= -WINDOW, 1, 0), 0)
        ok = jnp.where(kpos >= lo, jnp.where(kpos < hi, ok, 0), 0)
        valid.append(ok > 0)
    scores = [[jnp.where(valid[j], _dot(k_refs[j][0][r0:r1, :], q_ref[128 * h:128 * h + 128, :]), NEG_BIG)
               for j, (r0, r1) in enumerate(spans)] for h in range(4)]
    for h in range(4):
        kv = h // 2
        s = scores[h]
        sink = sink_ref[h] * LOG2E
        m = jnp.maximum(jnp.maximum(jnp.max(s[0], axis=0, keepdims=True), jnp.max(s[1], axis=0, keepdims=True)),
                        jnp.maximum(jnp.max(s[2], axis=0, keepdims=True), sink))
        l = jnp.exp2(sink - m)
        acc = jnp.zeros((HEAD_DIM, tq), F32)
        for j, (r0, r1) in enumerate(spans):
            p = jnp.exp2(s[j] - m)
            l = l + jnp.sum(p, axis=0, keepdims=True)
            acc = acc + _dot(v_refs[j][0][64 * kv:64 * kv + 64, r0:r1], p.astype(BF16))
        o_ref[64 * h:64 * h + 64, :] = (acc / l).astype(o_ref.dtype)


def _band_attn_call(sink, qT, k3, v3, seq, l):
    t = qT.shape[1]
    nb = t // TK
    clamp = lambda b: jnp.clip(b, 0, nb - 1)
    kspec = lambda off: pl.BlockSpec((1, TK, 128), lambda i: (clamp(i + off), 0, 0))
    vspec = lambda off: pl.BlockSpec((1, 128, TK), lambda i: (clamp(i + off), 0, 0))
    return pl.pallas_call(
        functools.partial(_band_attn_kernel, seq=seq),
        grid=(nb,),
        in_specs=[pl.BlockSpec(memory_space=pltpu.SMEM),
                  pl.BlockSpec((512, TK), lambda i: (0, i)),
                  kspec(-1), kspec(0), kspec(1), vspec(-1), vspec(0), vspec(1)],
        out_specs=pl.BlockSpec((256, TK), lambda i: (0, i)),
        out_shape=jax.ShapeDtypeStruct((256, t), BF16),
        compiler_params=pltpu.CompilerParams(dimension_semantics=("arbitrary",), vmem_limit_bytes=VMEM_LIMIT),
        name=f"attn_b_{l}",
    )(sink, qT, k3, k3, k3, v3, v3, v3)


def _merge_kernel(x_ref, oa_ref, ob_ref, oc_ref, od_ref, gpre_ref, gpost_ref, wg_ref, wb_ref, wout_ref, o_ref):
    x = x_ref[...]
    tm = x.shape[1]
    xn = (_rms_rows(x) * _lanes(gpre_ref[...], tm)).astype(BF16)
    merged = jnp.zeros((D_MODEL, tm), F32)
    for n, br_ref in enumerate((oa_ref, ob_ref, oc_ref, od_ref)):
        gl = _dot(wg_ref[n * D_MODEL:(n + 1) * D_MODEL, :], xn)
        bp = _dot(wb_ref[n], br_ref[...])
        merged = merged + _sigmoid(gl) * bp
    mix = _dot(wout_ref[...], merged.astype(BF16))
    o_ref[...] = x + _rms_rows(mix) * _lanes(gpost_ref[...], tm)


def _merge_call(xT, branches, gb, wgT, wbT, woutT, l):
    t = xT.shape[1]
    const = lambda *idx: (lambda i: idx)
    tok = lambda rows: pl.BlockSpec((rows, TM), lambda i: (0, i))
    return pl.pallas_call(
        _merge_kernel,
        grid=(t // TM,),
        in_specs=[
            tok(D_MODEL), tok(BRANCH_W), tok(BRANCH_W), tok(BRANCH_W), tok(BRANCH_W),
            pl.BlockSpec((None, None, D_MODEL, LANES), const(l, 2, 0, 0)),
            pl.BlockSpec((None, None, D_MODEL, LANES), const(l, 3, 0, 0)),
            pl.BlockSpec((None, N_BRANCH * D_MODEL, D_MODEL), const(l, 0, 0), pipeline_mode=pl.Buffered(1)),
            pl.BlockSpec((None, N_BRANCH, D_MODEL, BRANCH_W), const(l, 0, 0, 0), pipeline_mode=pl.Buffered(1)),
            pl.BlockSpec((None, D_MODEL, D_MODEL), const(l, 0, 0), pipeline_mode=pl.Buffered(1)),
        ],
        out_specs=tok(D_MODEL),
        out_shape=jax.ShapeDtypeStruct(xT.shape, F32),
        compiler_params=pltpu.CompilerParams(dimension_semantics=("arbitrary",), vmem_limit_bytes=VMEM_LIMIT),
        name=f"merge_{l}",
    )(xT, *branches, gb, gb, wgT, wbT, woutT)


def _rope_tables(pos):
    def cs(p, dim):
        inv = 1.0 / (ROPE_THETA ** (jnp.arange(0, dim, 2, dtype=F32) / dim))
        ang = p[:, None] * inv[None, :]
        ang = jnp.concatenate([ang, ang], axis=-1)
        sign = jnp.concatenate([-jnp.ones((dim // 2,), F32), jnp.ones((dim // 2,), F32)])
        return jnp.cos(ang).T, (jnp.sin(ang) * sign[None, :]).T
    row = jnp.floor(pos / GRID_W)
    col = pos - row * GRID_W
    cr, sr = cs(row, HEAD_DIM // 2)
    cc, sc = cs(col, HEAD_DIM // 2)
    c64, s64 = cs(pos, HEAD_DIM)
    c32, s32 = cs(pos, C_ROPE)
    return (jnp.concatenate([cr, cc], axis=0), jnp.concatenate([sr, sc], axis=0), c64, s64, c32, s32)


def kernel(x_prompt, x_sample, norm_g, w_in, a_qk_norm, b_sink, c_q_norm, c_kv_norm, c_w_uq, c_w_ukv,
           d_lambda, d_subln, w_branch, w_out, ffn_wi, ffn_wo):
    b_p, s_p, _ = x_prompt.shape
    b_s, s_s, _ = x_sample.shape
    t_p, t_s = b_p * s_p, b_s * s_s
    t = t_p + t_s
    depth = norm_g.shape[0]
    seq_unit = max(TM, TQC * N_CHAINS, PIPE_HS * PIPE_G * TK)
    assert s_p % seq_unit == 0 and s_s % seq_unit == 0
    seq = (t_p, s_p, s_s)

    xT = jnp.concatenate([x_prompt.reshape(t_p, D_MODEL), x_sample.reshape(t_s, D_MODEL)], axis=0).T

    pos = jnp.concatenate([jnp.tile(jnp.arange(s_p, dtype=F32), b_p), jnp.tile(jnp.arange(s_s, dtype=F32), b_s)])
    tabs = _rope_tables(pos)

    lane_bcast = lambda g: jnp.broadcast_to(g[..., None].astype(F32), g.shape + (LANES,))
    gb = lane_bcast(norm_g)
    aqkg = lane_bcast(a_qk_norm)
    cqg = lane_bcast(c_q_norm)
    ckvg = lane_bcast(c_kv_norm)
    dsg = lane_bcast(d_subln)
    tr = lambda w: jnp.swapaxes(w, -1, -2).astype(BF16)
    wiT, woT = tr(ffn_wi), tr(ffn_wo)
    winT, wgT = tr(w_in[:, :, :N_MIX_COLS]), tr(w_in[:, :, N_MIX_COLS:])
    wbT, woutT = tr(w_branch), tr(w_out)
    wuqT, wukvT = tr(c_w_uq), tr(c_w_ukv)
    sink = b_sink.astype(F32)
    dlam = d_lambda.astype(F32)

    nb = t // TK
    for l in range(depth):
        lambda_init = 0.8 - 0.6 * math.exp(-0.3 * l)
        xT = _ffn_call(xT, gb, wiT, woT, l, 0)
        (qa, ka, va, qb, kb, vb, qc, kc, vc, qd, kd, vd) = _prep_call(
            xT, gb, winT, aqkg, cqg, ckvg, wuqT, wukvT, tabs, l)
        k3 = lambda k: k.reshape(nb, TK, k.shape[1])
        oa = _dense_attn_call(qa, k3(ka), va, seq, n_steps=2, n_rows=2, k_block=lambda h: 0, name=f"attn_a_{l}")
        ob = _band_attn_call(sink[l], qb, k3(kb), vb, seq, l)
        oc = _dense_attn_call(qc, k3(kc), vc, seq, n_steps=4, n_rows=1, k_block=lambda h: h, name=f"attn_c_{l}")
        od = _dense_attn_call(qd, k3(kd), vd, seq, n_steps=4, n_rows=2, k_block=lambda h: h // 2,
                              name=f"attn_d_{l}", combine=True, extra=(dlam, dsg),
                              out_scale=1.0 - lambda_init, l=l)
        xT = _merge_call(xT, (oa, ob, oc, od), gb, wgT, wbT, woutT, l)
        xT = _ffn_call(xT, gb, wiT, woT, l, 1)

    y = xT.T
    return (y[:t_p].reshape(b_p, s_p, D_MODEL), y[t_p:].reshape(b_s, s_s, D_MODEL))
```

```python
import functools
import math

import numpy as np
import jax
import jax.numpy as jnp
from jax import lax
from jax.experimental import pallas as pl
from jax.experimental.pallas import tpu as pltpu

D_MODEL = 1024
GRID_W = 64
HEAD_DIM = 64
WINDOW = 128
C_NOPE = 64
C_ROPE = 32
C_Q_LORA = 384
C_KV_LORA = 256
D_HEAD_DIM = 32
N_BRANCH = 4
BRANCH_W = 256
D_FF = 2816
ROPE_THETA = 10000.0
EPS = 1e-6
NEG_BIG = -1e30
LOG2E = 1.4426950408889634
N_MIX_COLS = 2464

LANES = 128
TM = 512
TK = 256
TQC = 256
N_CHAINS = 4
PIPE_G = 1
PIPE_HS = 8
DV_PAD = 80
FFN_CHUNK = 1408
VMEM_LIMIT = 56 * 1024 * 1024

BF16 = jnp.bfloat16
F32 = jnp.float32


def _dot(a, b):
    return jnp.dot(a, b, preferred_element_type=F32)


def _lanes(gb, n):
    return jnp.tile(gb, (1, n // LANES))


def _rms_rows(x):
    return x * lax.rsqrt(jnp.mean(x * x, axis=0, keepdims=True) + EPS)


def _sigmoid(x):
    return 1.0 / (1.0 + jnp.exp(-x))


def _swap_halves(x, group):
    h = group // 2
    parts = []
    for r in range(0, x.shape[0], group):
        parts += [x[r + h:r + group], x[r:r + h]]
    return jnp.concatenate(parts, axis=0)


def _rope_rows(x, cos, sin_signed, group):
    n = x.shape[0] // cos.shape[0]
    if n > 1:
        cos = jnp.concatenate([cos] * n, axis=0)
        sin_signed = jnp.concatenate([sin_signed] * n, axis=0)
    return x * cos + _swap_halves(x, group) * sin_signed


def _seq_bounds(q0, t_p, s_p, s_s):
    lo_p = (q0 // s_p) * s_p
    lo_s = t_p + ((q0 - t_p) // s_s) * s_s
    in_p = q0 < t_p
    lo = jnp.where(in_p, lo_p, lo_s)
    hi = jnp.where(in_p, lo_p + s_p, lo_s + s_s)
    return lo, hi


def _ffn_kernel(x_ref, gpre_ref, gpost_ref, wi_ref, wo_ref, o_ref):
    x = x_ref[...]
    tm = x.shape[1]
    xn = (_rms_rows(x) * _lanes(gpre_ref[...], tm)).astype(BF16)
    acc = jnp.zeros((D_MODEL, tm), F32)
    for c in range(D_FF // FFN_CHUNK):
        lo = c * FFN_CHUNK
        hg = _dot(wi_ref[lo:lo + FFN_CHUNK, :], xn)
        hu = _dot(wi_ref[D_FF + lo:D_FF + lo + FFN_CHUNK, :], xn)
        a = (hg * _sigmoid(hg) * hu).astype(BF16)
        acc = acc + _dot(wo_ref[:, lo:lo + FFN_CHUNK], a)
    o_ref[...] = x + 0.5 * (_rms_rows(acc) * _lanes(gpost_ref[...], tm))


def _ffn_call(xT, gb, wiT, woT, l, j):
    t = xT.shape[1]
    const = lambda *idx: (lambda i: idx)
    return pl.pallas_call(
        _ffn_kernel,
        grid=(t // TM,),
        in_specs=[
            pl.BlockSpec((D_MODEL, TM), lambda i: (0, i)),
            pl.BlockSpec((None, None, D_MODEL, LANES), const(l, 4 * j, 0, 0)),
            pl.BlockSpec((None, None, D_MODEL, LANES), const(l, 4 * j + 1, 0, 0)),
            pl.BlockSpec((None, None, 2 * D_FF, D_MODEL), const(l, j, 0, 0), pipeline_mode=pl.Buffered(1)),
            pl.BlockSpec((None, None, D_MODEL, D_FF), const(l, j, 0, 0), pipeline_mode=pl.Buffered(1)),
        ],
        out_specs=pl.BlockSpec((D_MODEL, TM), lambda i: (0, i)),
        out_shape=jax.ShapeDtypeStruct(xT.shape, F32),
        compiler_params=pltpu.CompilerParams(dimension_semantics=("arbitrary",), vmem_limit_bytes=VMEM_LIMIT),
        name=f"ffn_{l}_{j}",
    )(xT, gb, gb, wiT, woT)


def _prep_kernel(x_ref, g_ref, win_ref, aqg_ref, akg_ref, cqg_ref, ckvg_ref, wuq_ref, wukv_ref,
                 cos_a_ref, sin_a_ref, cos64_ref, sin64_ref, cos32_ref, sin32_ref,
                 qa_ref, ka_ref, va_ref, qb_ref, kb_ref, vb_ref,
                 qc_ref, kc_ref, vc_ref, qd_ref, kd_ref, vd_ref):
    x = x_ref[...]
    tm = x.shape[1]
    xn = (_rms_rows(x) * _lanes(g_ref[...], tm)).astype(BF16)
    proj = _dot(win_ref[...], xn)
    aq, ak, av = proj[0:256], proj[256:384], proj[384:512]
    bq, bk, bv = proj[512:768], proj[768:896], proj[896:1024]
    cq, ckv, ckpe = proj[1024:1408], proj[1408:1664], proj[1664:1696]
    dq, dk, dv = proj[1696:1952], proj[1952:2208], proj[2208:2464]
    zeros32 = jnp.zeros((32, tm), F32)
    zeros64 = jnp.zeros((64, tm), F32)

    def store_vt(ref, vt):
        vt = vt.astype(BF16)
        for jj in range(tm // TK):
            ref[jj] = vt[:, jj * TK:(jj + 1) * TK]

    ones_rows = jnp.where(lax.broadcasted_iota(jnp.int32, (DV_PAD - HEAD_DIM, tm), 0) == 0, 1.0, 0.0)

    def with_ones(vt):
        parts = []
        for r in range(0, vt.shape[0], HEAD_DIM):
            parts += [vt[r:r + HEAD_DIM], ones_rows]
        return jnp.concatenate(parts, axis=0)

    def pad_kv(y, kv):
        return [y, zeros64] if kv == 0 else [zeros64, y]

    cos_a, sin_a = cos_a_ref[...], sin_a_ref[...]
    aqg, akg = _lanes(aqg_ref[...], tm), _lanes(akg_ref[...], tm)
    sc64 = HEAD_DIM ** -0.5 * LOG2E
    parts = []
    for h in range(4):
        y = _rms_rows(aq[64 * h:64 * h + 64]) * aqg
        y = _rope_rows(y, cos_a, sin_a, 32) * sc64
        parts += pad_kv(y, h // 2)
    qa_ref[...] = jnp.concatenate(parts, axis=0).astype(BF16)
    parts = []
    for kv in range(2):
        y = _rms_rows(ak[64 * kv:64 * kv + 64]) * akg
        parts.append(_rope_rows(y, cos_a, sin_a, 32))
    ka_ref[...] = jnp.concatenate(parts, axis=0).T.astype(BF16)
    store_vt(va_ref, with_ones(av))

    cos64, sin64 = cos64_ref[...], sin64_ref[...]
    yb = _rope_rows(bq, cos64, sin64, 64) * sc64
    parts = []
    for h in range(4):
        parts += pad_kv(yb[64 * h:64 * h + 64], h // 2)
    qb_ref[...] = jnp.concatenate(parts, axis=0).astype(BF16)
    kb_ref[...] = _rope_rows(bk, cos64, sin64, 64).T.astype(BF16)
    store_vt(vb_ref, bv)

    cos32, sin32 = cos32_ref[...], sin32_ref[...]
    cqn = (_rms_rows(cq) * _lanes(cqg_ref[...], tm)).astype(BF16)
    cqh = _dot(wuq_ref[...], cqn)
    kvn = (_rms_rows(ckv) * _lanes(ckvg_ref[...], tm)).astype(BF16)
    kvh = _dot(wukv_ref[...], kvn)
    kpe = _rope_rows(ckpe, cos32, sin32, 32)
    sc96 = (C_NOPE + C_ROPE) ** -0.5 * LOG2E
    qparts, kparts, vparts = [], [], []
    for h in range(4):
        qn = cqh[96 * h:96 * h + 64]
        qp = _rope_rows(cqh[96 * h + 64:96 * h + 96], cos32, sin32, 32)
        qparts += [qn * sc96, qp * sc96, zeros32]
        kparts += [kvh[128 * h:128 * h + 64], kpe, zeros32]
        vparts.append(kvh[128 * h + 64:128 * h + 128])
    qc_ref[...] = jnp.concatenate(qparts, axis=0).astype(BF16)
    kc_ref[...] = jnp.concatenate(kparts, axis=0).T.astype(BF16)
    store_vt(vc_ref, with_ones(jnp.concatenate(vparts, axis=0)))

    sc32 = D_HEAD_DIM ** -0.5 * LOG2E
    yd = _rope_rows(dq, cos32, sin32, 32) * sc32
    parts = []
    for m in range(8):
        blk = [zeros32] * 4
        blk[m % 4] = yd[32 * m:32 * m + 32]
        parts += blk
    qd_ref[...] = jnp.concatenate(parts, axis=0).astype(BF16)
    kd_ref[...] = _rope_rows(dk, cos32, sin32, 32).T.astype(BF16)
    store_vt(vd_ref, with_ones(dv))


def _prep_call(xT, gb, winT, aqkg, cqg, ckvg, wuqT, wukvT, tabs, l):
    t = xT.shape[1]
    nb = t // TK
    const = lambda *idx: (lambda i: idx)
    tok = lambda rows: pl.BlockSpec((rows, TM), lambda i: (0, i))
    nat = lambda cols: pl.BlockSpec((TM, cols), lambda i: (i, 0))
    vts = lambda rows: pl.BlockSpec((TM // TK, rows, TK), lambda i: (i, 0, 0))
    sds = jax.ShapeDtypeStruct
    out_shape = [
        sds((512, t), BF16), sds((t, 128), BF16), sds((nb, 2 * DV_PAD, TK), BF16),
        sds((512, t), BF16), sds((t, 128), BF16), sds((nb, 128, TK), BF16),
        sds((512, t), BF16), sds((t, 512), BF16), sds((nb, 4 * DV_PAD, TK), BF16),
        sds((1024, t), BF16), sds((t, 256), BF16), sds((nb, 4 * DV_PAD, TK), BF16),
    ]
    out_specs = [tok(512), nat(128), vts(2 * DV_PAD), tok(512), nat(128), vts(128),
                 tok(512), nat(512), vts(4 * DV_PAD), tok(1024), nat(256), vts(4 * DV_PAD)]
    in_specs = [
        tok(D_MODEL),
        pl.BlockSpec((None, None, D_MODEL, LANES), const(l, 2, 0, 0)),
        pl.BlockSpec((None, N_MIX_COLS, D_MODEL), const(l, 0, 0), pipeline_mode=pl.Buffered(1)),
        pl.BlockSpec((None, None, HEAD_DIM, LANES), const(l, 0, 0, 0)),
        pl.BlockSpec((None, None, HEAD_DIM, LANES), const(l, 1, 0, 0)),
        pl.BlockSpec((None, C_Q_LORA, LANES), const(l, 0, 0)),
        pl.BlockSpec((None, C_KV_LORA, LANES), const(l, 0, 0)),
        pl.BlockSpec((None, 384, C_Q_LORA), const(l, 0, 0)),
        pl.BlockSpec((None, 512, C_KV_LORA), const(l, 0, 0)),
        tok(64), tok(64), tok(64), tok(64), tok(32), tok(32),
    ]
    return pl.pallas_call(
        _prep_kernel,
        grid=(t // TM,),
        in_specs=in_specs,
        out_specs=out_specs,
        out_shape=out_shape,
        compiler_params=pltpu.CompilerParams(dimension_semantics=("arbitrary",), vmem_limit_bytes=VMEM_LIMIT),
        name=f"prep_{l}",
    )(xT, gb, winT, aqkg, aqkg, cqg, ckvg, wuqT, wukvT, *tabs)


def _dense_attn_kernel(*refs, n_rows, combine, seq, out_scale):
    if combine:
        q_ref, k_ref, v_ref, lam_ref, g_ref, o_ref, s_ref, p_ref, al_ref, m_ref, acc_ref, mb_ref = refs
    else:
        q_ref, k_ref, v_ref, o_ref, s_ref, p_ref, al_ref, m_ref, acc_ref, mb_ref = refs
    tq = q_ref.shape[1]
    n_cg = N_CHAINS // n_rows
    lo, hi = _seq_bounds(pl.program_id(1) * tq, *seq)
    lo_b = lo // TK
    hi_b = hi // TK
    chains = [(r, cg) for r in range(n_rows) for cg in range(n_cg)]
    g2 = PIPE_G
    hs = PIPE_HS

    def q_of(c):
        r, cg = chains[c]
        return q_ref[128 * r:128 * r + 128, TQC * cg:TQC * cg + TQC]

    def mxu_stage(kb_pv, half, kb_s, slot, do_pv=True, do_s=True):
        for g in range(g2):
            if do_pv:
                vblk = v_ref[kb_pv + g]
            if do_s:
                kblk = k_ref[kb_s + g]
            for c in range(N_CHAINS):
                if do_s:
                    sv = _dot(kblk, q_of(c))
                    s_ref[slot, g, c] = sv
                    mb_ref[slot, g, c] = jnp.max(sv, axis=0, keepdims=True)
                if do_pv:
                    acc_ref[c] = al_ref[half, g, c] * acc_ref[c] + _dot(vblk, p_ref[half, g, c])

    def softmax_stage(slot, half):
        for g in range(g2):
            for c in range(N_CHAINS):
                s = s_ref[slot, g, c]
                m = m_ref[c]
                m_new = jnp.maximum(m, mb_ref[slot, g, c])
                al_ref[half, g, c] = jnp.exp2(m - m_new)
                p_ref[half, g, c] = jnp.exp2(s - m_new).astype(BF16)
                m_ref[c] = m_new

    def trip(kb, first, last):
        for j in range(hs):
            mxu_stage(kb + (j - hs) * g2, j, kb + (j + 1) * g2, (j + 1) % hs, do_pv=not first,
                      do_s=not (last and j == hs - 1))
            softmax_stage(j, j)

    for c in range(N_CHAINS):
        m_ref[c] = jnp.full((1, TQC), NEG_BIG, F32)
        acc_ref[c] = jnp.zeros((DV_PAD, TQC), F32)
    mxu_stage(0, 0, lo_b, 0, do_pv=False)

    n_trips = (hi_b - lo_b) // (hs * g2)

    def single_trip():
        trip(lo_b, True, True)

    def multi_trip():
        trip(lo_b, True, False)

        def body(it, carry):
            trip(lo_b + hs * g2 * it, False, False)
            return carry

        lax.fori_loop(1, n_trips - 1, body, 0)
        trip(hi_b - hs * g2, False, True)

    lax.cond(n_trips == 1, single_trip, multi_trip)
    for j in range(hs):
        mxu_stage(hi_b - (hs - j) * g2, j, 0, 0, do_s=False)

    outs = []
    for c in range(N_CHAINS):
        acc = acc_ref[c]
        outs.append(acc[0:HEAD_DIM] / acc[HEAD_DIM:HEAD_DIM + 1])
    if not combine:
        for c, (r, cg) in enumerate(chains):
            o_ref[HEAD_DIM * r:HEAD_DIM * (r + 1), TQC * cg:TQC * (cg + 1)] = outs[c].astype(o_ref.dtype)
    else:
        lf = lam_ref[...]
        lam = (jnp.exp(jnp.sum(lf[0:1] * lf[1:2], axis=1, keepdims=True))
               - jnp.exp(jnp.sum(lf[2:3] * lf[3:4], axis=1, keepdims=True)) + (1.0 - out_scale))
        gain = _lanes(g_ref[...], TQC)
        for cg in range(n_cg):
            o = outs[cg] - lam * outs[n_cg + cg]
            o_ref[:, TQC * cg:TQC * (cg + 1)] = (_rms_rows(o) * gain * out_scale).astype(o_ref.dtype)


def _dense_attn_call(qT, k3, v3, seq, *, n_steps, n_rows, k_block, name, combine=False, extra=(), out_scale=1.0,
                     l=0):
    t = qT.shape[1]
    nb = k3.shape[0]
    tq = TQC * (N_CHAINS // n_rows)
    out_rows = HEAD_DIM if combine else HEAD_DIM * n_rows
    in_specs = [
        pl.BlockSpec((128 * n_rows, tq), lambda h, i: (h, i)),
        pl.BlockSpec((nb, TK, 128), lambda h, i: (0, 0, k_block(h))),
        pl.BlockSpec((nb, DV_PAD, TK), lambda h, i: (0, h, 0)),
    ]
    if combine:
        in_specs += [pl.BlockSpec((None, 4, D_HEAD_DIM), lambda h, i: (l, 0, 0)),
                     pl.BlockSpec((None, 2 * D_HEAD_DIM, LANES), lambda h, i: (l, 0, 0))]
    tiles = (PIPE_HS, PIPE_G, N_CHAINS)
    return pl.pallas_call(
        functools.partial(_dense_attn_kernel, n_rows=n_rows, combine=combine, seq=seq, out_scale=out_scale),
        grid=(n_steps, t // tq),
        in_specs=in_specs,
        out_specs=pl.BlockSpec((out_rows, tq), lambda h, i: (h, i)),
        out_shape=jax.ShapeDtypeStruct((n_steps * out_rows, t), BF16),
        scratch_shapes=[pltpu.VMEM(tiles + (TK, TQC), F32), pltpu.VMEM(tiles + (TK, TQC), BF16),
                        pltpu.VMEM(tiles + (1, TQC), F32), pltpu.VMEM((N_CHAINS, 1, TQC), F32),
                        pltpu.VMEM((N_CHAINS, DV_PAD, TQC), F32), pltpu.VMEM(tiles + (1, TQC), F32)],
        compiler_params=pltpu.CompilerParams(dimension_semantics=("arbitrary", "arbitrary"),
                                             vmem_limit_bytes=VMEM_LIMIT),
        name=name,
    )(qT, k3, v3, *extra)


def _band_attn_kernel(sink_ref, q_ref, k0_ref, k1_ref, k2_ref, v0_ref, v1_ref, v2_ref, o_ref, *, seq):
    i = pl.program_id(0)
    tq = q_ref.shape[1]
    lo, hi = _seq_bounds(i * tq, *seq)
    spans = ((TK - WINDOW, TK), (0, TK), (0, WINDOW))
    k_refs = (k0_ref, k1_ref, k2_ref)
    v_refs = (v0_ref, v1_ref, v2_ref)
    valid = []
    for j, (r0, r1) in enumerate(spans):
        shape = (r1 - r0, tq)
        qpos = i * tq + lax.broadcasted_iota(jnp.int32, shape, 1)
        kpos = (i - 1 + j) * TK + r0 + lax.broadcasted_iota(jnp.int32, shape, 0)
        d = kpos - qpos
        ok = jnp.where(d <= WINDOW, jnp.where(d >= -WINDOW, 1, 0), 0)
        ok = jnp.where(kpos >= lo, jnp.where(kpos < hi, ok, 0), 0)
        valid.append(ok > 0)
    scores = [[jnp.where(valid[j], _dot(k_refs[j][0][r0:r1, :], q_ref[128 * h:128 * h + 128, :]), NEG_BIG)
               for j, (r0, r1) in enumerate(spans)] for h in range(4)]
    for h in range(4):
        kv = h // 2
        s = scores[h]
        sink = sink_ref[h] * LOG2E
        m = jnp.maximum(jnp.maximum(jnp.max(s[0], axis=0, keepdims=True), jnp.max(s[1], axis=0, keepdims=True)),
                        jnp.maximum(jnp.max(s[2], axis=0, keepdims=True), sink))
        l = jnp.exp2(sink - m)
        acc = jnp.zeros((HEAD_DIM, tq), F32)
        for j, (r0, r1) in enumerate(spans):
            p = jnp.exp2(s[j] - m)
            l = l + jnp.sum(p, axis=0, keepdims=True)
            acc = acc + _dot(v_refs[j][0][64 * kv:64 * kv + 64, r0:r1], p.astype(BF16))
        o_ref[64 * h:64 * h + 64, :] = (acc / l).astype(o_ref.dtype)


def _band_attn_call(sink, qT, k3, v3, seq, l):
    t = qT.shape[1]
    nb = t // TK
    clamp = lambda b: jnp.clip(b, 0, nb - 1)
    kspec = lambda off: pl.BlockSpec((1, TK, 128), lambda i: (clamp(i + off), 0, 0))
    vspec = lambda off: pl.BlockSpec((1, 128, TK), lambda i: (clamp(i + off), 0, 0))
    return pl.pallas_call(
        functools.partial(_band_attn_kernel, seq=seq),
        grid=(nb,),
        in_specs=[pl.BlockSpec(memory_space=pltpu.SMEM),
                  pl.BlockSpec((512, TK), lambda i: (0, i)),
                  kspec(-1), kspec(0), kspec(1), vspec(-1), vspec(0), vspec(1)],
        out_specs=pl.BlockSpec((256, TK), lambda i: (0, i)),
        out_shape=jax.ShapeDtypeStruct((256, t), BF16),
        compiler_params=pltpu.CompilerParams(dimension_semantics=("arbitrary",), vmem_limit_bytes=VMEM_LIMIT),
        name=f"attn_b_{l}",
    )(sink, qT, k3, k3, k3, v3, v3, v3)


def _merge_kernel(x_ref, oa_ref, ob_ref, oc_ref, od_ref, gpre_ref, gpost_ref, wg_ref, wb_ref, wout_ref, o_ref):
    x = x_ref[...]
    tm = x.shape[1]
    xn = (_rms_rows(x) * _lanes(gpre_ref[...], tm)).astype(BF16)
    merged = jnp.zeros((D_MODEL, tm), F32)
    for n, br_ref in enumerate((oa_ref, ob_ref, oc_ref, od_ref)):
        gl = _dot(wg_ref[n * D_MODEL:(n + 1) * D_MODEL, :], xn)
        bp = _dot(wb_ref[n], br_ref[...])
        merged = merged + _sigmoid(gl) * bp
    mix = _dot(wout_ref[...], merged.astype(BF16))
    o_ref[...] = x + _rms_rows(mix) * _lanes(gpost_ref[...], tm)


def _merge_call(xT, branches, gb, wgT, wbT, woutT, l):
    t = xT.shape[1]
    const = lambda *idx: (lambda i: idx)
    tok = lambda rows: pl.BlockSpec((rows, TM), lambda i: (0, i))
    return pl.pallas_call(
        _merge_kernel,
        grid=(t // TM,),
        in_specs=[
            tok(D_MODEL), tok(BRANCH_W), tok(BRANCH_W), tok(BRANCH_W), tok(BRANCH_W),
            pl.BlockSpec((None, None, D_MODEL, LANES), const(l, 2, 0, 0)),
            pl.BlockSpec((None, None, D_MODEL, LANES), const(l, 3, 0, 0)),
            pl.BlockSpec((None, N_BRANCH * D_MODEL, D_MODEL), const(l, 0, 0), pipeline_mode=pl.Buffered(1)),
            pl.BlockSpec((None, N_BRANCH, D_MODEL, BRANCH_W), const(l, 0, 0, 0), pipeline_mode=pl.Buffered(1)),
            pl.BlockSpec((None, D_MODEL, D_MODEL), const(l, 0, 0), pipeline_mode=pl.Buffered(1)),
        ],
        out_specs=tok(D_MODEL),
        out_shape=jax.ShapeDtypeStruct(xT.shape, F32),
        compiler_params=pltpu.CompilerParams(dimension_semantics=("arbitrary",), vmem_limit_bytes=VMEM_LIMIT),
        name=f"merge_{l}",
    )(xT, *branches, gb, gb, wgT, wbT, woutT)


def _rope_tables(pos):
    def cs(p, dim):
        inv = 1.0 / (ROPE_THETA ** (jnp.arange(0, dim, 2, dtype=F32) / dim))
        ang = p[:, None] * inv[None, :]
        ang = jnp.concatenate([ang, ang], axis=-1)
        sign = jnp.concatenate([-jnp.ones((dim // 2,), F32), jnp.ones((dim // 2,), F32)])
        return jnp.cos(ang).T, (jnp.sin(ang) * sign[None, :]).T
    row = jnp.floor(pos / GRID_W)
    col = pos - row * GRID_W
    cr, sr = cs(row, HEAD_DIM // 2)
    cc, sc = cs(col, HEAD_DIM // 2)
    c64, s64 = cs(pos, HEAD_DIM)
    c32, s32 = cs(pos, C_ROPE)
    return (jnp.concatenate([cr, cc], axis=0), jnp.concatenate([sr, sc], axis=0), c64, s64, c32, s32)


def kernel(x_prompt, x_sample, norm_g, w_in, a_qk_norm, b_sink, c_q_norm, c_kv_norm, c_w_uq, c_w_ukv,
           d_lambda, d_subln, w_branch, w_out, ffn_wi, ffn_wo):
    b_p, s_p, _ = x_prompt.shape
    b_s, s_s, _ = x_sample.shape
    t_p, t_s = b_p * s_p, b_s * s_s
    t = t_p + t_s
    depth = norm_g.shape[0]
    seq_unit = max(TM, TQC * N_CHAINS, PIPE_HS * PIPE_G * TK)
    assert s_p % seq_unit == 0 and s_s % seq_unit == 0
    seq = (t_p, s_p, s_s)

    xT = jnp.concatenate([x_prompt.reshape(t_p, D_MODEL), x_sample.reshape(t_s, D_MODEL)], axis=0).T

    pos = jnp.concatenate([jnp.tile(jnp.arange(s_p, dtype=F32), b_p), jnp.tile(jnp.arange(s_s, dtype=F32), b_s)])
    tabs = _rope_tables(pos)

    lane_bcast = lambda g: jnp.broadcast_to(g[..., None].astype(F32), g.shape + (LANES,))
    gb = lane_bcast(norm_g)
    aqkg = lane_bcast(a_qk_norm)
    cqg = lane_bcast(c_q_norm)
    ckvg = lane_bcast(c_kv_norm)
    dsg = lane_bcast(d_subln)
    tr = lambda w: jnp.swapaxes(w, -1, -2).astype(BF16)
    wiT, woT = tr(ffn_wi), tr(ffn_wo)
    winT, wgT = tr(w_in[:, :, :N_MIX_COLS]), tr(w_in[:, :, N_MIX_COLS:])
    wbT, woutT = tr(w_branch), tr(w_out)
    wuqT, wukvT = tr(c_w_uq), tr(c_w_ukv)
    sink = b_sink.astype(F32)
    dlam = d_lambda.astype(F32)

    nb = t // TK
    for l in range(depth):
        lambda_init = 0.8 - 0.6 * math.exp(-0.3 * l)
        xT = _ffn_call(xT, gb, wiT, woT, l, 0)
        (qa, ka, va, qb, kb, vb, qc, kc, vc, qd, kd, vd) = _prep_call(
            xT, gb, winT, aqkg, cqg, ckvg, wuqT, wukvT, tabs, l)
        k3 = lambda k: k.reshape(nb, TK, k.shape[1])
        oa = _dense_attn_call(qa, k3(ka), va, seq, n_steps=2, n_rows=2, k_block=lambda h: 0, name=f"attn_a_{l}")
        ob = _band_attn_call(sink[l], qb, k3(kb), vb, seq, l)
        oc = _dense_attn_call(qc, k3(kc), vc, seq, n_steps=4, n_rows=1, k_block=lambda h: h, name=f"attn_c_{l}")
        od = _dense_attn_call(qd, k3(kd), vd, seq, n_steps=4, n_rows=2, k_block=lambda h: h // 2,
                              name=f"attn_d_{l}", combine=True, extra=(dlam, dsg),
                              out_scale=1.0 - lambda_init, l=l)
        xT = _merge_call(xT, (oa, ob, oc, od), gb, wgT, wbT, woutT, l)
        xT = _ffn_call(xT, gb, wiT, woT, l, 1)

    y = xT.T
    return (y[:t_p].reshape(b_p, s_p, D_MODEL), y[t_p:].reshape(b_s, s_s, D_MODEL))
```

```python
import functools
import math

import numpy as np
import jax
import jax.numpy as jnp
from jax import lax
from jax.experimental import pallas as pl
from jax.experimental.pallas import tpu as pltpu

D_MODEL = 1024
GRID_W = 64
HEAD_DIM = 64
WINDOW = 128
C_NOPE = 64
C_ROPE = 32
C_Q_LORA = 384
C_KV_LORA = 256
D_HEAD_DIM = 32
N_BRANCH = 4
BRANCH_W = 256
D_FF = 2816
ROPE_THETA = 10000.0
EPS = 1e-6
NEG_BIG = -1e30
LOG2E = 1.4426950408889634
N_MIX_COLS = 2464

LANES = 128
TM = 512
TK = 256
TQC = 256
N_CHAINS = 4
PIPE_G = 1
PIPE_HS = 8
DV_PAD = 80
FFN_CHUNK = 1408
VMEM_LIMIT = 56 * 1024 * 1024

BF16 = jnp.bfloat16
F32 = jnp.float32


def _dot(a, b):
    return jnp.dot(a, b, preferred_element_type=F32)


def _lanes(gb, n):
    return jnp.tile(gb, (1, n // LANES))


def _rms_rows(x):
    return x * lax.rsqrt(jnp.mean(x * x, axis=0, keepdims=True) + EPS)


def _sigmoid(x):
    return 1.0 / (1.0 + jnp.exp(-x))


def _swap_halves(x, group):
    h = group // 2
    parts = []
    for r in range(0, x.shape[0], group):
        parts += [x[r + h:r + group], x[r:r + h]]
    return jnp.concatenate(parts, axis=0)


def _rope_rows(x, cos, sin_signed, group):
    n = x.shape[0] // cos.shape[0]
    if n > 1:
        cos = jnp.concatenate([cos] * n, axis=0)
        sin_signed = jnp.concatenate([sin_signed] * n, axis=0)
    return x * cos + _swap_halves(x, group) * sin_signed


def _seq_bounds(q0, t_p, s_p, s_s):
    lo_p = (q0 // s_p) * s_p
    lo_s = t_p + ((q0 - t_p) // s_s) * s_s
    in_p = q0 < t_p
    lo = jnp.where(in_p, lo_p, lo_s)
    hi = jnp.where(in_p, lo_p + s_p, lo_s + s_s)
    return lo, hi


def _ffn_kernel(x_ref, gpre_ref, gpost_ref, wi_ref, wo_ref, o_ref):
    x = x_ref[...]
    tm = x.shape[1]
    xn = (_rms_rows(x) * _lanes(gpre_ref[...], tm)).astype(BF16)
    acc = jnp.zeros((D_MODEL, tm), F32)
    for c in range(D_FF // FFN_CHUNK):
        lo = c * FFN_CHUNK
        hg = _dot(wi_ref[lo:lo + FFN_CHUNK, :], xn)
        hu = _dot(wi_ref[D_FF + lo:D_FF + lo + FFN_CHUNK, :], xn)
        a = (hg * _sigmoid(hg) * hu).astype(BF16)
        acc = acc + _dot(wo_ref[:, lo:lo + FFN_CHUNK], a)
    o_ref[...] = x + 0.5 * (_rms_rows(acc) * _lanes(gpost_ref[...], tm))


def _ffn_call(xT, gb, wiT, woT, l, j):
    t = xT.shape[1]
    const = lambda *idx: (lambda i: idx)
    return pl.pallas_call(
        _ffn_kernel,
        grid=(t // TM,),
        in_specs=[
            pl.BlockSpec((D_MODEL, TM), lambda i: (0, i)),
            pl.BlockSpec((None, None, D_MODEL, LANES), const(l, 4 * j, 0, 0)),
            pl.BlockSpec((None, None, D_MODEL, LANES), const(l, 4 * j + 1, 0, 0)),
            pl.BlockSpec((None, None, 2 * D_FF, D_MODEL), const(l, j, 0, 0), pipeline_mode=pl.Buffered(1)),
            pl.BlockSpec((None, None, D_MODEL, D_FF), const(l, j, 0, 0), pipeline_mode=pl.Buffered(1)),
        ],
        out_specs=pl.BlockSpec((D_MODEL, TM), lambda i: (0, i)),
        out_shape=jax.ShapeDtypeStruct(xT.shape, F32),
        compiler_params=pltpu.CompilerParams(dimension_semantics=("arbitrary",), vmem_limit_bytes=VMEM_LIMIT),
        name=f"ffn_{l}_{j}",
    )(xT, gb, gb, wiT, woT)


def _prep_kernel(x_ref, g_ref, win_ref, aqg_ref, akg_ref, cqg_ref, ckvg_ref, wuq_ref, wukv_ref,
                 cos_a_ref, sin_a_ref, cos64_ref, sin64_ref, cos32_ref, sin32_ref,
                 qa_ref, ka_ref, va_ref, qb_ref, kb_ref, vb_ref,
                 qc_ref, kc_ref, vc_ref, qd_ref, kd_ref, vd_ref):
    x = x_ref[...]
    tm = x.shape[1]
    xn = (_rms_rows(x) * _lanes(g_ref[...], tm)).astype(BF16)
    proj = _dot(win_ref[...], xn)
    aq, ak, av = proj[0:256], proj[256:384], proj[384:512]
    bq, bk, bv = proj[512:768], proj[768:896], proj[896:1024]
    cq, ckv, ckpe = proj[1024:1408], proj[1408:1664], proj[1664:1696]
    dq, dk, dv = proj[1696:1952], proj[1952:2208], proj[2208:2464]
    zeros32 = jnp.zeros((32, tm), F32)
    zeros64 = jnp.zeros((64, tm), F32)

    def store_vt(ref, vt):
        vt = vt.astype(BF16)
        for jj in range(tm // TK):
            ref[jj] = vt[:, jj * TK:(jj + 1) * TK]

    ones_rows = jnp.where(lax.broadcasted_iota(jnp.int32, (DV_PAD - HEAD_DIM, tm), 0) == 0, 1.0, 0.0)

    def with_ones(vt):
        parts = []
        for r in range(0, vt.shape[0], HEAD_DIM):
            parts += [vt[r:r + HEAD_DIM], ones_rows]
        return jnp.concatenate(parts, axis=0)

    def pad_kv(y, kv):
        return [y, zeros64] if kv == 0 else [zeros64, y]

    cos_a, sin_a = cos_a_ref[...], sin_a_ref[...]
    aqg, akg = _lanes(aqg_ref[...], tm), _lanes(akg_ref[...], tm)
    sc64 = HEAD_DIM ** -0.5 * LOG2E
    parts = []
    for h in range(4):
        y = _rms_rows(aq[64 * h:64 * h + 64]) * aqg
        y = _rope_rows(y, cos_a, sin_a, 32) * sc64
        parts += pad_kv(y, h // 2)
    qa_ref[...] = jnp.concatenate(parts, axis=0).astype(BF16)
    parts = []
    for kv in range(2):
        y = _rms_rows(ak[64 * kv:64 * kv + 64]) * akg
        parts.append(_rope_rows(y, cos_a, sin_a, 32))
    ka_ref[...] = jnp.concatenate(parts, axis=0).T.astype(BF16)
    store_vt(va_ref, with_ones(av))

    cos64, sin64 = cos64_ref[...], sin64_ref[...]
    yb = _rope_rows(bq, cos64, sin64, 64) * sc64
    parts = []
    for h in range(4):
        parts += pad_kv(yb[64 * h:64 * h + 64], h // 2)
    qb_ref[...] = jnp.concatenate(parts, axis=0).astype(BF16)
    kb_ref[...] = _rope_rows(bk, cos64, sin64, 64).T.astype(BF16)
    store_vt(vb_ref, bv)

    cos32, sin32 = cos32_ref[...], sin32_ref[...]
    cqn = (_rms_rows(cq) * _lanes(cqg_ref[...], tm)).astype(BF16)
    cqh = _dot(wuq_ref[...], cqn)
    kvn = (_rms_rows(ckv) * _lanes(ckvg_ref[...], tm)).astype(BF16)
    kvh = _dot(wukv_ref[...], kvn)
    kpe = _rope_rows(ckpe, cos32, sin32, 32)
    sc96 = (C_NOPE + C_ROPE) ** -0.5 * LOG2E
    qparts, kparts, vparts = [], [], []
    for h in range(4):
        qn = cqh[96 * h:96 * h + 64]
        qp = _rope_rows(cqh[96 * h + 64:96 * h + 96], cos32, sin32, 32)
        qparts += [qn * sc96, qp * sc96, zeros32]
        kparts += [kvh[128 * h:128 * h + 64], kpe, zeros32]
        vparts.append(kvh[128 * h + 64:128 * h + 128])
    qc_ref[...] = jnp.concatenate(qparts, axis=0).astype(BF16)
    kc_ref[...] = jnp.concatenate(kparts, axis=0).T.astype(BF16)
    store_vt(vc_ref, with_ones(jnp.concatenate(vparts, axis=0)))

    sc32 = D_HEAD_DIM ** -0.5 * LOG2E
    yd = _rope_rows(dq, cos32, sin32, 32) * sc32
    parts = []
    for m in range(8):
        blk = [zeros32] * 4
        blk[m % 4] = yd[32 * m:32 * m + 32]
        parts += blk
    qd_ref[...] = jnp.concatenate(parts, axis=0).astype(BF16)
    kd_ref[...] = _rope_rows(dk, cos32, sin32, 32).T.astype(BF16)
    store_vt(vd_ref, with_ones(dv))


def _prep_call(xT, gb, winT, aqkg, cqg, ckvg, wuqT, wukvT, tabs, l):
    t = xT.shape[1]
    nb = t // TK
    const = lambda *idx: (lambda i: idx)
    tok = lambda rows: pl.BlockSpec((rows, TM), lambda i: (0, i))
    nat = lambda cols: pl.BlockSpec((TM, cols), lambda i: (i, 0))
    vts = lambda rows: pl.BlockSpec((TM // TK, rows, TK), lambda i: (i, 0, 0))
    sds = jax.ShapeDtypeStruct
    out_shape = [
        sds((512, t), BF16), sds((t, 128), BF16), sds((nb, 2 * DV_PAD, TK), BF16),
        sds((512, t), BF16), sds((t, 128), BF16), sds((nb, 128, TK), BF16),
        sds((512, t), BF16), sds((t, 512), BF16), sds((nb, 4 * DV_PAD, TK), BF16),
        sds((1024, t), BF16), sds((t, 256), BF16), sds((nb, 4 * DV_PAD, TK), BF16),
    ]
    out_specs = [tok(512), nat(128), vts(2 * DV_PAD), tok(512), nat(128), vts(128),
                 tok(512), nat(512), vts(4 * DV_PAD), tok(1024), nat(256), vts(4 * DV_PAD)]
    in_specs = [
        tok(D_MODEL),
        pl.BlockSpec((None, None, D_MODEL, LANES), const(l, 2, 0, 0)),
        pl.BlockSpec((None, N_MIX_COLS, D_MODEL), const(l, 0, 0), pipeline_mode=pl.Buffered(1)),
        pl.BlockSpec((None, None, HEAD_DIM, LANES), const(l, 0, 0, 0)),
        pl.BlockSpec((None, None, HEAD_DIM, LANES), const(l, 1, 0, 0)),
        pl.BlockSpec((None, C_Q_LORA, LANES), const(l, 0, 0)),
        pl.BlockSpec((None, C_KV_LORA, LANES), const(l, 0, 0)),
        pl.BlockSpec((None, 384, C_Q_LORA), const(l, 0, 0)),
        pl.BlockSpec((None, 512, C_KV_LORA), const(l, 0, 0)),
        tok(64), tok(64), tok(64), tok(64), tok(32), tok(32),
    ]
    return pl.pallas_call(
        _prep_kernel,
        grid=(t // TM,),
        in_specs=in_specs,
        out_specs=out_specs,
        out_shape=out_shape,
        compiler_params=pltpu.CompilerParams(dimension_semantics=("arbitrary",), vmem_limit_bytes=VMEM_LIMIT),
        name=f"prep_{l}",
    )(xT, gb, winT, aqkg, aqkg, cqg, ckvg, wuqT, wukvT, *tabs)


def _dense_attn_kernel(*refs, n_rows, combine, seq, out_scale):
    if combine:
        q_ref, k_ref, v_ref, lam_ref, g_ref, o_ref, s_ref, p_ref, al_ref, m_ref, acc_ref, mb_ref = refs
    else:
        q_ref, k_ref, v_ref, o_ref, s_ref, p_ref, al_ref, m_ref, acc_ref, mb_ref = refs
    tq = q_ref.shape[1]
    n_cg = N_CHAINS // n_rows
    lo, hi = _seq_bounds(pl.program_id(1) * tq, *seq)
    lo_b = lo // TK
    hi_b = hi // TK
    chains = [(r, cg) for r in range(n_rows) for cg in range(n_cg)]
    g2 = PIPE_G
    hs = PIPE_HS

    def q_of(c):
        r, cg = chains[c]
        return q_ref[128 * r:128 * r + 128, TQC * cg:TQC * cg + TQC]

    def mxu_stage(kb_pv, half, kb_s, slot, do_pv=True, do_s=True):
        for g in range(g2):
            if do_pv:
                vblk = v_ref[kb_pv + g]
            if do_s:
                kblk = k_ref[kb_s + g]
            for c in range(N_CHAINS):
                if do_s:
                    sv = _dot(kblk, q_of(c))
                    s_ref[slot, g, c] = sv
                    mb_ref[slot, g, c] = jnp.max(sv, axis=0, keepdims=True)
                if do_pv:
                    acc_ref[c] = al_ref[half, g, c] * acc_ref[c] + _dot(vblk, p_ref[half, g, c])

    def softmax_stage(slot, half):
        for g in range(g2):
            for c in range(N_CHAINS):
                s = s_ref[slot, g, c]
                m = m_ref[c]
                m_new = jnp.maximum(m, mb_ref[slot, g, c])
                al_ref[half, g, c] = jnp.exp2(m - m_new)
                p_ref[half, g, c] = jnp.exp2(s - m_new).astype(BF16)
                m_ref[c] = m_new

    def trip(kb, first, last, n):
        for j in range(n):
            mxu_stage(kb + (j - n) * g2, j, kb + (j + 1) * g2, (j + 1) % n, do_pv=not first,
                      do_s=not (last and j == n - 1))
            softmax_stage(j, j)

    def drain(n):
        for j in range(n):
            mxu_stage(hi_b - (n - j) * g2, j, 0, 0, do_s=False)

    for c in range(N_CHAINS):
        m_ref[c] = jnp.full((1, TQC), NEG_BIG, F32)
        acc_ref[c] = jnp.zeros((DV_PAD, TQC), F32)
    mxu_stage(0, 0, lo_b, 0, do_pv=False)

    n_trips = (hi_b - lo_b) // (hs * g2)
    half = hs // 2

    def short_sequence():
        trip(lo_b, True, False, half)
        trip(lo_b + half * g2, False, True, half)
        drain(half)

    def long_sequence():
        trip(lo_b, True, False, hs)

        def body(it, carry):
            trip(lo_b + hs * g2 * it, False, False, hs)
            return carry

        lax.fori_loop(1, n_trips - 1, body, 0)
        trip(hi_b - hs * g2, False, True, hs)
        drain(hs)

    lax.cond(n_trips == 1, short_sequence, long_sequence)

    outs = []
    for c in range(N_CHAINS):
        acc = acc_ref[c]
        outs.append(acc[0:HEAD_DIM] / acc[HEAD_DIM:HEAD_DIM + 1])
    if not combine:
        for c, (r, cg) in enumerate(chains):
            o_ref[HEAD_DIM * r:HEAD_DIM * (r + 1), TQC * cg:TQC * (cg + 1)] = outs[c].astype(o_ref.dtype)
    else:
        lf = lam_ref[...]
        lam = (jnp.exp(jnp.sum(lf[0:1] * lf[1:2], axis=1, keepdims=True))
               - jnp.exp(jnp.sum(lf[2:3] * lf[3:4], axis=1, keepdims=True)) + (1.0 - out_scale))
        gain = _lanes(g_ref[...], TQC)
        for cg in range(n_cg):
            o = outs[cg] - lam * outs[n_cg + cg]
            o_ref[:, TQC * cg:TQC * (cg + 1)] = (_rms_rows(o) * gain * out_scale).astype(o_ref.dtype)


def _dense_attn_call(qT, k3, v3, seq, *, n_steps, n_rows, k_block, name, combine=False, extra=(), out_scale=1.0,
                     l=0):
    t = qT.shape[1]
    nb = k3.shape[0]
    tq = TQC * (N_CHAINS // n_rows)
    out_rows = HEAD_DIM if combine else HEAD_DIM * n_rows
    in_specs = [
        pl.BlockSpec((128 * n_rows, tq), lambda h, i: (h, i)),
        pl.BlockSpec((nb, TK, 128), lambda h, i: (0, 0, k_block(h))),
        pl.BlockSpec((nb, DV_PAD, TK), lambda h, i: (0, h, 0)),
    ]
    if combine:
        in_specs += [pl.BlockSpec((None, 4, D_HEAD_DIM), lambda h, i: (l, 0, 0)),
                     pl.BlockSpec((None, 2 * D_HEAD_DIM, LANES), lambda h, i: (l, 0, 0))]
    tiles = (PIPE_HS, PIPE_G, N_CHAINS)
    return pl.pallas_call(
        functools.partial(_dense_attn_kernel, n_rows=n_rows, combine=combine, seq=seq, out_scale=out_scale),
        grid=(n_steps, t // tq),
        in_specs=in_specs,
        out_specs=pl.BlockSpec((out_rows, tq), lambda h, i: (h, i)),
        out_shape=jax.ShapeDtypeStruct((n_steps * out_rows, t), BF16),
        scratch_shapes=[pltpu.VMEM(tiles + (TK, TQC), F32), pltpu.VMEM(tiles + (TK, TQC), BF16),
                        pltpu.VMEM(tiles + (1, TQC), F32), pltpu.VMEM((N_CHAINS, 1, TQC), F32),
                        pltpu.VMEM((N_CHAINS, DV_PAD, TQC), F32), pltpu.VMEM(tiles + (1, TQC), F32)],
        compiler_params=pltpu.CompilerParams(dimension_semantics=("arbitrary", "arbitrary"),
                                             vmem_limit_bytes=VMEM_LIMIT),
        name=name,
    )(qT, k3, v3, *extra)


def _band_attn_kernel(sink_ref, q_ref, k0_ref, k1_ref, k2_ref, v0_ref, v1_ref, v2_ref, o_ref, *, seq):
    i = pl.program_id(0)
    tq = q_ref.shape[1]
    lo, hi = _seq_bounds(i * tq, *seq)
    spans = ((TK - WINDOW, TK), (0, TK), (0, WINDOW))
    k_refs = (k0_ref, k1_ref, k2_ref)
    v_refs = (v0_ref, v1_ref, v2_ref)
    valid = []
    for j, (r0, r1) in enumerate(spans):
        shape = (r1 - r0, tq)
        qpos = i * tq + lax.broadcasted_iota(jnp.int32, shape, 1)
        kpos = (i - 1 + j) * TK + r0 + lax.broadcasted_iota(jnp.int32, shape, 0)
        d = kpos - qpos
        ok = jnp.where(d <= WINDOW, jnp.where(d >= -WINDOW, 1, 0), 0)
        ok = jnp.where(kpos >= lo, jnp.where(kpos < hi, ok, 0), 0)
        valid.append(ok > 0)
    scores = [[jnp.where(valid[j], _dot(k_refs[j][0][r0:r1, :], q_ref[128 * h:128 * h + 128, :]), NEG_BIG)
               for j, (r0, r1) in enumerate(spans)] for h in range(4)]
    for h in range(4):
        kv = h // 2
        s = scores[h]
        sink = sink_ref[h] * LOG2E
        m = jnp.maximum(jnp.maximum(jnp.max(s[0], axis=0, keepdims=True), jnp.max(s[1], axis=0, keepdims=True)),
                        jnp.maximum(jnp.max(s[2], axis=0, keepdims=True), sink))
        l = jnp.exp2(sink - m)
        acc = jnp.zeros((HEAD_DIM, tq), F32)
        for j, (r0, r1) in enumerate(spans):
            p = jnp.exp2(s[j] - m)
            l = l + jnp.sum(p, axis=0, keepdims=True)
            acc = acc + _dot(v_refs[j][0][64 * kv:64 * kv + 64, r0:r1], p.astype(BF16))
        o_ref[64 * h:64 * h + 64, :] = (acc / l).astype(o_ref.dtype)


def _band_attn_call(sink, qT, k3, v3, seq, l):
    t = qT.shape[1]
    nb = t // TK
    clamp = lambda b: jnp.clip(b, 0, nb - 1)
    kspec = lambda off: pl.BlockSpec((1, TK, 128), lambda i: (clamp(i + off), 0, 0))
    vspec = lambda off: pl.BlockSpec((1, 128, TK), lambda i: (clamp(i + off), 0, 0))
    return pl.pallas_call(
        functools.partial(_band_attn_kernel, seq=seq),
        grid=(nb,),
        in_specs=[pl.BlockSpec(memory_space=pltpu.SMEM),
                  pl.BlockSpec((512, TK), lambda i: (0, i)),
                  kspec(-1), kspec(0), kspec(1), vspec(-1), vspec(0), vspec(1)],
        out_specs=pl.BlockSpec((256, TK), lambda i: (0, i)),
        out_shape=jax.ShapeDtypeStruct((256, t), BF16),
        compiler_params=pltpu.CompilerParams(dimension_semantics=("arbitrary",), vmem_limit_bytes=VMEM_LIMIT),
        name=f"attn_b_{l}",
    )(sink, qT, k3, k3, k3, v3, v3, v3)


def _merge_kernel(x_ref, oa_ref, ob_ref, oc_ref, od_ref, gpre_ref, gpost_ref, wg_ref, wb_ref, wout_ref, o_ref):
    x = x_ref[...]
    tm = x.shape[1]
    xn = (_rms_rows(x) * _lanes(gpre_ref[...], tm)).astype(BF16)
    merged = jnp.zeros((D_MODEL, tm), F32)
    for n, br_ref in enumerate((oa_ref, ob_ref, oc_ref, od_ref)):
        gl = _dot(wg_ref[n * D_MODEL:(n + 1) * D_MODEL, :], xn)
        bp = _dot(wb_ref[n], br_ref[...])
        merged = merged + _sigmoid(gl) * bp
    mix = _dot(wout_ref[...], merged.astype(BF16))
    o_ref[...] = x + _rms_rows(mix) * _lanes(gpost_ref[...], tm)


def _merge_call(xT, branches, gb, wgT, wbT, woutT, l):
    t = xT.shape[1]
    const = lambda *idx: (lambda i: idx)
    tok = lambda rows: pl.BlockSpec((rows, TM), lambda i: (0, i))
    return pl.pallas_call(
        _merge_kernel,
        grid=(t // TM,),
        in_specs=[
            tok(D_MODEL), tok(BRANCH_W), tok(BRANCH_W), tok(BRANCH_W), tok(BRANCH_W),
            pl.BlockSpec((None, None, D_MODEL, LANES), const(l, 2, 0, 0)),
            pl.BlockSpec((None, None, D_MODEL, LANES), const(l, 3, 0, 0)),
            pl.BlockSpec((None, N_BRANCH * D_MODEL, D_MODEL), const(l, 0, 0), pipeline_mode=pl.Buffered(1)),
            pl.BlockSpec((None, N_BRANCH, D_MODEL, BRANCH_W), const(l, 0, 0, 0), pipeline_mode=pl.Buffered(1)),
            pl.BlockSpec((None, D_MODEL, D_MODEL), const(l, 0, 0), pipeline_mode=pl.Buffered(1)),
        ],
        out_specs=tok(D_MODEL),
        out_shape=jax.ShapeDtypeStruct(xT.shape, F32),
        compiler_params=pltpu.CompilerParams(dimension_semantics=("arbitrary",), vmem_limit_bytes=VMEM_LIMIT),
        name=f"merge_{l}",
    )(xT, *branches, gb, gb, wgT, wbT, woutT)


def _rope_tables(pos):
    def cs(p, dim):
        inv = 1.0 / (ROPE_THETA ** (jnp.arange(0, dim, 2, dtype=F32) / dim))
        ang = p[:, None] * inv[None, :]
        ang = jnp.concatenate([ang, ang], axis=-1)
        sign = jnp.concatenate([-jnp.ones((dim // 2,), F32), jnp.ones((dim // 2,), F32)])
        return jnp.cos(ang).T, (jnp.sin(ang) * sign[None, :]).T
    row = jnp.floor(pos / GRID_W)
    col = pos - row * GRID_W
    cr, sr = cs(row, HEAD_DIM // 2)
    cc, sc = cs(col, HEAD_DIM // 2)
    c64, s64 = cs(pos, HEAD_DIM)
    c32, s32 = cs(pos, C_ROPE)
    return (jnp.concatenate([cr, cc], axis=0), jnp.concatenate([sr, sc], axis=0), c64, s64, c32, s32)


def kernel(x_prompt, x_sample, norm_g, w_in, a_qk_norm, b_sink, c_q_norm, c_kv_norm, c_w_uq, c_w_ukv,
           d_lambda, d_subln, w_branch, w_out, ffn_wi, ffn_wo):
    b_p, s_p, _ = x_prompt.shape
    b_s, s_s, _ = x_sample.shape
    t_p, t_s = b_p * s_p, b_s * s_s
    t = t_p + t_s
    depth = norm_g.shape[0]
    seq_unit = max(TM, TQC * N_CHAINS, PIPE_HS * PIPE_G * TK)
    assert s_p % seq_unit == 0 and s_s % seq_unit == 0
    seq = (t_p, s_p, s_s)

    xT = jnp.concatenate([x_prompt.reshape(t_p, D_MODEL), x_sample.reshape(t_s, D_MODEL)], axis=0).T

    pos = jnp.concatenate([jnp.tile(jnp.arange(s_p, dtype=F32), b_p), jnp.tile(jnp.arange(s_s, dtype=F32), b_s)])
    tabs = _rope_tables(pos)

    lane_bcast = lambda g: jnp.broadcast_to(g[..., None].astype(F32), g.shape + (LANES,))
    gb = lane_bcast(norm_g)
    aqkg = lane_bcast(a_qk_norm)
    cqg = lane_bcast(c_q_norm)
    ckvg = lane_bcast(c_kv_norm)
    dsg = lane_bcast(d_subln)
    tr = lambda w: jnp.swapaxes(w, -1, -2).astype(BF16)
    wiT, woT = tr(ffn_wi), tr(ffn_wo)
    winT, wgT = tr(w_in[:, :, :N_MIX_COLS]), tr(w_in[:, :, N_MIX_COLS:])
    wbT, woutT = tr(w_branch), tr(w_out)
    wuqT, wukvT = tr(c_w_uq), tr(c_w_ukv)
    sink = b_sink.astype(F32)
    dlam = d_lambda.astype(F32)

    nb = t // TK
    for l in range(depth):
        lambda_init = 0.8 - 0.6 * math.exp(-0.3 * l)
        xT = _ffn_call(xT, gb, wiT, woT, l, 0)
        (qa, ka, va, qb, kb, vb, qc, kc, vc, qd, kd, vd) = _prep_call(
            xT, gb, winT, aqkg, cqg, ckvg, wuqT, wukvT, tabs, l)
        k3 = lambda k: k.reshape(nb, TK, k.shape[1])
        oa = _dense_attn_call(qa, k3(ka), va, seq, n_steps=2, n_rows=2, k_block=lambda h: 0, name=f"attn_a_{l}")
        ob = _band_attn_call(sink[l], qb, k3(kb), vb, seq, l)
        oc = _dense_attn_call(qc, k3(kc), vc, seq, n_steps=4, n_rows=1, k_block=lambda h: h, name=f"attn_c_{l}")
        od = _dense_attn_call(qd, k3(kd), vd, seq, n_steps=4, n_rows=2, k_block=lambda h: h // 2,
                              name=f"attn_d_{l}", combine=True, extra=(dlam, dsg),
                              out_scale=1.0 - lambda_init, l=l)
        xT = _merge_call(xT, (oa, ob, oc, od), gb, wgT, wbT, woutT, l)
        xT = _ffn_call(xT, gb, wiT, woT, l, 1)

    y = xT.T
    return (y[:t_p].reshape(b_p, s_p, D_MODEL), y[t_p:].reshape(b_s, s_s, D_MODEL))
```

```python
import functools
import math

import numpy as np
import jax
import jax.numpy as jnp
from jax import lax
from jax.experimental import pallas as pl
from jax.experimental.pallas import tpu as pltpu

D_MODEL = 1024
GRID_W = 64
HEAD_DIM = 64
WINDOW = 128
C_NOPE = 64
C_ROPE = 32
C_Q_LORA = 384
C_KV_LORA = 256
D_HEAD_DIM = 32
N_BRANCH = 4
BRANCH_W = 256
D_FF = 2816
ROPE_THETA = 10000.0
EPS = 1e-6
NEG_BIG = -1e30
LOG2E = 1.4426950408889634
N_MIX_COLS = 2464

LANES = 128
TM = 512
TM_FFN = 1024
TK = 256
TQC = 256
N_CHAINS = 4
PIPE_G = 1
PIPE_HS = 8
DV_PAD = 80
FFN_CHUNK = 1408
VMEM_LIMIT = 56 * 1024 * 1024

BF16 = jnp.bfloat16
F32 = jnp.float32


def _dot(a, b):
    return jnp.dot(a, b, preferred_element_type=F32)


def _lanes(gb, n):
    return jnp.tile(gb, (1, n // LANES))


def _rms_rows(x):
    return x * lax.rsqrt(jnp.mean(x * x, axis=0, keepdims=True) + EPS)


def _sigmoid(x):
    return 1.0 / (1.0 + jnp.exp(-x))


def _swap_halves(x, group):
    h = group // 2
    parts = []
    for r in range(0, x.shape[0], group):
        parts += [x[r + h:r + group], x[r:r + h]]
    return jnp.concatenate(parts, axis=0)


def _rope_rows(x, cos, sin_signed, group):
    n = x.shape[0] // cos.shape[0]
    if n > 1:
        cos = jnp.concatenate([cos] * n, axis=0)
        sin_signed = jnp.concatenate([sin_signed] * n, axis=0)
    return x * cos + _swap_halves(x, group) * sin_signed


def _seq_bounds(q0, t_p, s_p, s_s):
    lo_p = (q0 // s_p) * s_p
    lo_s = t_p + ((q0 - t_p) // s_s) * s_s
    in_p = q0 < t_p
    lo = jnp.where(in_p, lo_p, lo_s)
    hi = jnp.where(in_p, lo_p + s_p, lo_s + s_s)
    return lo, hi


def _ffn_kernel(x_ref, gpre_ref, gpost_ref, wi_ref, wo_ref, o_ref):
    x = x_ref[...]
    tm = x.shape[1]
    xn = (_rms_rows(x) * _lanes(gpre_ref[...], tm)).astype(BF16)
    acc = jnp.zeros((D_MODEL, tm), F32)
    for c in range(D_FF // FFN_CHUNK):
        lo = c * FFN_CHUNK
        hg = _dot(wi_ref[lo:lo + FFN_CHUNK, :], xn)
        hu = _dot(wi_ref[D_FF + lo:D_FF + lo + FFN_CHUNK, :], xn)
        a = (hg * _sigmoid(hg) * hu).astype(BF16)
        acc = acc + _dot(wo_ref[:, lo:lo + FFN_CHUNK], a)
    o_ref[...] = x + 0.5 * (_rms_rows(acc) * _lanes(gpost_ref[...], tm))


def _ffn_call(xT, gb, wiT, woT, l, j):
    t = xT.shape[1]
    const = lambda *idx: (lambda i: idx)
    return pl.pallas_call(
        _ffn_kernel,
        grid=(t // TM_FFN,),
        in_specs=[
            pl.BlockSpec((D_MODEL, TM_FFN), lambda i: (0, i)),
            pl.BlockSpec((None, None, D_MODEL, LANES), const(l, 4 * j, 0, 0)),
            pl.BlockSpec((None, None, D_MODEL, LANES), const(l, 4 * j + 1, 0, 0)),
            pl.BlockSpec((None, None, 2 * D_FF, D_MODEL), const(l, j, 0, 0), pipeline_mode=pl.Buffered(1)),
            pl.BlockSpec((None, None, D_MODEL, D_FF), const(l, j, 0, 0), pipeline_mode=pl.Buffered(1)),
        ],
        out_specs=pl.BlockSpec((D_MODEL, TM_FFN), lambda i: (0, i)),
        out_shape=jax.ShapeDtypeStruct(xT.shape, F32),
        compiler_params=pltpu.CompilerParams(dimension_semantics=("arbitrary",), vmem_limit_bytes=VMEM_LIMIT),
        name=f"ffn_{l}_{j}",
    )(xT, gb, gb, wiT, woT)


def _prep_kernel(x_ref, g_ref, win_ref, aqg_ref, akg_ref, cqg_ref, ckvg_ref, wuq_ref, wukv_ref,
                 cos_a_ref, sin_a_ref, cos64_ref, sin64_ref, cos32_ref, sin32_ref,
                 qa_ref, ka_ref, va_ref, qb_ref, kb_ref, vb_ref,
                 qc_ref, kc_ref, vc_ref, qd_ref, kd_ref, vd_ref):
    x = x_ref[...]
    tm = x.shape[1]
    xn = (_rms_rows(x) * _lanes(g_ref[...], tm)).astype(BF16)
    proj = _dot(win_ref[...], xn)
    aq, ak, av = proj[0:256], proj[256:384], proj[384:512]
    bq, bk, bv = proj[512:768], proj[768:896], proj[896:1024]
    cq, ckv, ckpe = proj[1024:1408], proj[1408:1664], proj[1664:1696]
    dq, dk, dv = proj[1696:1952], proj[1952:2208], proj[2208:2464]
    zeros32 = jnp.zeros((32, tm), F32)
    zeros64 = jnp.zeros((64, tm), F32)

    def store_vt(ref, vt):
        vt = vt.astype(BF16)
        for jj in range(tm // TK):
            ref[jj] = vt[:, jj * TK:(jj + 1) * TK]

    ones_rows = jnp.where(lax.broadcasted_iota(jnp.int32, (DV_PAD - HEAD_DIM, tm), 0) == 0, 1.0, 0.0)

    def with_ones(vt):
        parts = []
        for r in range(0, vt.shape[0], HEAD_DIM):
            parts += [vt[r:r + HEAD_DIM], ones_rows]
        return jnp.concatenate(parts, axis=0)

    def pad_kv(y, kv):
        return [y, zeros64] if kv == 0 else [zeros64, y]

    cos_a, sin_a = cos_a_ref[...], sin_a_ref[...]
    aqg, akg = _lanes(aqg_ref[...], tm), _lanes(akg_ref[...], tm)
    sc64 = HEAD_DIM ** -0.5 * LOG2E
    parts = []
    for h in range(4):
        y = _rms_rows(aq[64 * h:64 * h + 64]) * aqg
        y = _rope_rows(y, cos_a, sin_a, 32) * sc64
        parts += pad_kv(y, h // 2)
    qa_ref[...] = jnp.concatenate(parts, axis=0).astype(BF16)
    parts = []
    for kv in range(2):
        y = _rms_rows(ak[64 * kv:64 * kv + 64]) * akg
        parts.append(_rope_rows(y, cos_a, sin_a, 32))
    ka_ref[...] = jnp.concatenate(parts, axis=0).T.astype(BF16)
    store_vt(va_ref, with_ones(av))

    cos64, sin64 = cos64_ref[...], sin64_ref[...]
    yb = _rope_rows(bq, cos64, sin64, 64) * sc64
    parts = []
    for h in range(4):
        parts += pad_kv(yb[64 * h:64 * h + 64], h // 2)
    qb_ref[...] = jnp.concatenate(parts, axis=0).astype(BF16)
    kb_ref[...] = _rope_rows(bk, cos64, sin64, 64).T.astype(BF16)
    store_vt(vb_ref, bv)

    cos32, sin32 = cos32_ref[...], sin32_ref[...]
    cqn = (_rms_rows(cq) * _lanes(cqg_ref[...], tm)).astype(BF16)
    cqh = _dot(wuq_ref[...], cqn)
    kvn = (_rms_rows(ckv) * _lanes(ckvg_ref[...], tm)).astype(BF16)
    kvh = _dot(wukv_ref[...], kvn)
    kpe = _rope_rows(ckpe, cos32, sin32, 32)
    sc96 = (C_NOPE + C_ROPE) ** -0.5 * LOG2E
    qparts, kparts, vparts = [], [], []
    for h in range(4):
        qn = cqh[96 * h:96 * h + 64]
        qp = _rope_rows(cqh[96 * h + 64:96 * h + 96], cos32, sin32, 32)
        qparts += [qn * sc96, qp * sc96, zeros32]
        kparts += [kvh[128 * h:128 * h + 64], kpe, zeros32]
        vparts.append(kvh[128 * h + 64:128 * h + 128])
    qc_ref[...] = jnp.concatenate(qparts, axis=0).astype(BF16)
    kc_ref[...] = jnp.concatenate(kparts, axis=0).T.astype(BF16)
    store_vt(vc_ref, with_ones(jnp.concatenate(vparts, axis=0)))

    sc32 = D_HEAD_DIM ** -0.5 * LOG2E
    yd = _rope_rows(dq, cos32, sin32, 32) * sc32
    parts = []
    for m in range(8):
        blk = [zeros32] * 4
        blk[m % 4] = yd[32 * m:32 * m + 32]
        parts += blk
    qd_ref[...] = jnp.concatenate(parts, axis=0).astype(BF16)
    kd_ref[...] = _rope_rows(dk, cos32, sin32, 32).T.astype(BF16)
    store_vt(vd_ref, with_ones(dv))


def _prep_call(xT, gb, winT, aqkg, cqg, ckvg, wuqT, wukvT, tabs, l):
    t = xT.shape[1]
    nb = t // TK
    const = lambda *idx: (lambda i: idx)
    tok = lambda rows: pl.BlockSpec((rows, TM), lambda i: (0, i))
    nat = lambda cols: pl.BlockSpec((TM, cols), lambda i: (i, 0))
    vts = lambda rows: pl.BlockSpec((TM // TK, rows, TK), lambda i: (i, 0, 0))
    sds = jax.ShapeDtypeStruct
    out_shape = [
        sds((512, t), BF16), sds((t, 128), BF16), sds((nb, 2 * DV_PAD, TK), BF16),
        sds((512, t), BF16), sds((t, 128), BF16), sds((nb, 128, TK), BF16),
        sds((512, t), BF16), sds((t, 512), BF16), sds((nb, 4 * DV_PAD, TK), BF16),
        sds((1024, t), BF16), sds((t, 256), BF16), sds((nb, 4 * DV_PAD, TK), BF16),
    ]
    out_specs = [tok(512), nat(128), vts(2 * DV_PAD), tok(512), nat(128), vts(128),
                 tok(512), nat(512), vts(4 * DV_PAD), tok(1024), nat(256), vts(4 * DV_PAD)]
    in_specs = [
        tok(D_MODEL),
        pl.BlockSpec((None, None, D_MODEL, LANES), const(l, 2, 0, 0)),
        pl.BlockSpec((None, N_MIX_COLS, D_MODEL), const(l, 0, 0), pipeline_mode=pl.Buffered(1)),
        pl.BlockSpec((None, None, HEAD_DIM, LANES), const(l, 0, 0, 0)),
        pl.BlockSpec((None, None, HEAD_DIM, LANES), const(l, 1, 0, 0)),
        pl.BlockSpec((None, C_Q_LORA, LANES), const(l, 0, 0)),
        pl.BlockSpec((None, C_KV_LORA, LANES), const(l, 0, 0)),
        pl.BlockSpec((None, 384, C_Q_LORA), const(l, 0, 0)),
        pl.BlockSpec((None, 512, C_KV_LORA), const(l, 0, 0)),
        tok(64), tok(64), tok(64), tok(64), tok(32), tok(32),
    ]
    return pl.pallas_call(
        _prep_kernel,
        grid=(t // TM,),
        in_specs=in_specs,
        out_specs=out_specs,
        out_shape=out_shape,
        compiler_params=pltpu.CompilerParams(dimension_semantics=("arbitrary",), vmem_limit_bytes=VMEM_LIMIT),
        name=f"prep_{l}",
    )(xT, gb, winT, aqkg, aqkg, cqg, ckvg, wuqT, wukvT, *tabs)


def _dense_attn_kernel(*refs, n_rows, combine, seq, out_scale):
    if combine:
        q_ref, k_ref, v_ref, lam_ref, g_ref, o_ref, s_ref, p_ref, al_ref, m_ref, acc_ref, mb_ref = refs
    else:
        q_ref, k_ref, v_ref, o_ref, s_ref, p_ref, al_ref, m_ref, acc_ref, mb_ref = refs
    tq = q_ref.shape[1]
    n_cg = N_CHAINS // n_rows
    lo, hi = _seq_bounds(pl.program_id(1) * tq, *seq)
    lo_b = lo // TK
    hi_b = hi // TK
    chains = [(r, cg) for r in range(n_rows) for cg in range(n_cg)]
    g2 = PIPE_G
    hs = PIPE_HS

    def q_of(c):
        r, cg = chains[c]
        return q_ref[128 * r:128 * r + 128, TQC * cg:TQC * cg + TQC]

    def mxu_stage(kb_pv, half, kb_s, slot, do_pv=True, do_s=True):
        for g in range(g2):
            if do_pv:
                vblk = v_ref[kb_pv + g]
            if do_s:
                kblk = k_ref[kb_s + g]
            for c in range(N_CHAINS):
                if do_s:
                    sv = _dot(kblk, q_of(c))
                    s_ref[slot, g, c] = sv
                    mb_ref[slot, g, c] = jnp.max(sv, axis=0, keepdims=True)
                if do_pv:
                    acc_ref[c] = al_ref[half, g, c] * acc_ref[c] + _dot(vblk, p_ref[half, g, c])

    def softmax_stage(slot, half):
        for g in range(g2):
            for c in range(N_CHAINS):
                s = s_ref[slot, g, c]
                m = m_ref[c]
                m_new = jnp.maximum(m, mb_ref[slot, g, c])
                al_ref[half, g, c] = jnp.exp2(m - m_new)
                p_ref[half, g, c] = jnp.exp2(s - m_new).astype(BF16)
                m_ref[c] = m_new

    def trip(kb, first, last, n):
        for j in range(n):
            mxu_stage(kb + (j - n) * g2, j, kb + (j + 1) * g2, (j + 1) % n, do_pv=not first,
                      do_s=not (last and j == n - 1))
            softmax_stage(j, j)

    def drain(n):
        for j in range(n):
            mxu_stage(hi_b - (n - j) * g2, j, 0, 0, do_s=False)

    for c in range(N_CHAINS):
        m_ref[c] = jnp.full((1, TQC), NEG_BIG, F32)
        acc_ref[c] = jnp.zeros((DV_PAD, TQC), F32)
    mxu_stage(0, 0, lo_b, 0, do_pv=False)

    n_trips = (hi_b - lo_b) // (hs * g2)
    half = hs // 2

    def short_sequence():
        trip(lo_b, True, False, half)
        trip(lo_b + half * g2, False, True, half)
        drain(half)

    def long_sequence():
        trip(lo_b, True, False, hs)

        def body(it, carry):
            trip(lo_b + hs * g2 * it, False, False, hs)
            return carry

        lax.fori_loop(1, n_trips - 1, body, 0)
        trip(hi_b - hs * g2, False, True, hs)
        drain(hs)

    lax.cond(n_trips == 1, short_sequence, long_sequence)

    outs = []
    for c in range(N_CHAINS):
        acc = acc_ref[c]
        outs.append(acc[0:HEAD_DIM] / acc[HEAD_DIM:HEAD_DIM + 1])
    if not combine:
        for c, (r, cg) in enumerate(chains):
            o_ref[HEAD_DIM * r:HEAD_DIM * (r + 1), TQC * cg:TQC * (cg + 1)] = outs[c].astype(o_ref.dtype)
    else:
        lf = lam_ref[...]
        lam = (jnp.exp(jnp.sum(lf[0:1] * lf[1:2], axis=1, keepdims=True))
               - jnp.exp(jnp.sum(lf[2:3] * lf[3:4], axis=1, keepdims=True)) + (1.0 - out_scale))
        gain = _lanes(g_ref[...], TQC)
        for cg in range(n_cg):
            o = outs[cg] - lam * outs[n_cg + cg]
            o_ref[:, TQC * cg:TQC * (cg + 1)] = (_rms_rows(o) * gain * out_scale).astype(o_ref.dtype)


def _dense_attn_call(qT, k3, v3, seq, *, n_steps, n_rows, k_block, name, combine=False, extra=(), out_scale=1.0,
                     l=0):
    t = qT.shape[1]
    nb = k3.shape[0]
    tq = TQC * (N_CHAINS // n_rows)
    out_rows = HEAD_DIM if combine else HEAD_DIM * n_rows
    in_specs = [
        pl.BlockSpec((128 * n_rows, tq), lambda h, i: (h, i)),
        pl.BlockSpec((nb, TK, 128), lambda h, i: (0, 0, k_block(h))),
        pl.BlockSpec((nb, DV_PAD, TK), lambda h, i: (0, h, 0)),
    ]
    if combine:
        in_specs += [pl.BlockSpec((None, 4, D_HEAD_DIM), lambda h, i: (l, 0, 0)),
                     pl.BlockSpec((None, 2 * D_HEAD_DIM, LANES), lambda h, i: (l, 0, 0))]
    tiles = (PIPE_HS, PIPE_G, N_CHAINS)
    return pl.pallas_call(
        functools.partial(_dense_attn_kernel, n_rows=n_rows, combine=combine, seq=seq, out_scale=out_scale),
        grid=(n_steps, t // tq),
        in_specs=in_specs,
        out_specs=pl.BlockSpec((out_rows, tq), lambda h, i: (h, i)),
        out_shape=jax.ShapeDtypeStruct((n_steps * out_rows, t), BF16),
        scratch_shapes=[pltpu.VMEM(tiles + (TK, TQC), F32), pltpu.VMEM(tiles + (TK, TQC), BF16),
                        pltpu.VMEM(tiles + (1, TQC), F32), pltpu.VMEM((N_CHAINS, 1, TQC), F32),
                        pltpu.VMEM((N_CHAINS, DV_PAD, TQC), F32), pltpu.VMEM(tiles + (1, TQC), F32)],
        compiler_params=pltpu.CompilerParams(dimension_semantics=("arbitrary", "arbitrary"),
                                             vmem_limit_bytes=VMEM_LIMIT),
        name=name,
    )(qT, k3, v3, *extra)


def _band_attn_kernel(sink_ref, q_ref, k0_ref, k1_ref, k2_ref, v0_ref, v1_ref, v2_ref, o_ref, *, seq):
    i = pl.program_id(0)
    tq = q_ref.shape[1]
    lo, hi = _seq_bounds(i * tq, *seq)
    spans = ((TK - WINDOW, TK), (0, TK), (0, WINDOW))
    k_refs = (k0_ref, k1_ref, k2_ref)
    v_refs = (v0_ref, v1_ref, v2_ref)
    valid = []
    for j, (r0, r1) in enumerate(spans):
        shape = (r1 - r0, tq)
        qpos = i * tq + lax.broadcasted_iota(jnp.int32, shape, 1)
        kpos = (i - 1 + j) * TK + r0 + lax.broadcasted_iota(jnp.int32, shape, 0)
        d = kpos - qpos
        ok = jnp.where(d <= WINDOW, jnp.where(d >= -WINDOW, 1, 0), 0)
        ok = jnp.where(kpos >= lo, jnp.where(kpos < hi, ok, 0), 0)
        valid.append(ok > 0)
    scores = [[jnp.where(valid[j], _dot(k_refs[j][0][r0:r1, :], q_ref[128 * h:128 * h + 128, :]), NEG_BIG)
               for j, (r0, r1) in enumerate(spans)] for h in range(4)]
    for h in range(4):
        kv = h // 2
        s = scores[h]
        sink = sink_ref[h] * LOG2E
        m = jnp.maximum(jnp.maximum(jnp.max(s[0], axis=0, keepdims=True), jnp.max(s[1], axis=0, keepdims=True)),
                        jnp.maximum(jnp.max(s[2], axis=0, keepdims=True), sink))
        l = jnp.exp2(sink - m)
        acc = jnp.zeros((HEAD_DIM, tq), F32)
        for j, (r0, r1) in enumerate(spans):
            p = jnp.exp2(s[j] - m)
            l = l + jnp.sum(p, axis=0, keepdims=True)
            acc = acc + _dot(v_refs[j][0][64 * kv:64 * kv + 64, r0:r1], p.astype(BF16))
        o_ref[64 * h:64 * h + 64, :] = (acc / l).astype(o_ref.dtype)


def _band_attn_call(sink, qT, k3, v3, seq, l):
    t = qT.shape[1]
    nb = t // TK
    clamp = lambda b: jnp.clip(b, 0, nb - 1)
    kspec = lambda off: pl.BlockSpec((1, TK, 128), lambda i: (clamp(i + off), 0, 0))
    vspec = lambda off: pl.BlockSpec((1, 128, TK), lambda i: (clamp(i + off), 0, 0))
    return pl.pallas_call(
        functools.partial(_band_attn_kernel, seq=seq),
        grid=(nb,),
        in_specs=[pl.BlockSpec(memory_space=pltpu.SMEM),
                  pl.BlockSpec((512, TK), lambda i: (0, i)),
                  kspec(-1), kspec(0), kspec(1), vspec(-1), vspec(0), vspec(1)],
        out_specs=pl.BlockSpec((256, TK), lambda i: (0, i)),
        out_shape=jax.ShapeDtypeStruct((256, t), BF16),
        compiler_params=pltpu.CompilerParams(dimension_semantics=("arbitrary",), vmem_limit_bytes=VMEM_LIMIT),
        name=f"attn_b_{l}",
    )(sink, qT, k3, k3, k3, v3, v3, v3)


def _merge_kernel(x_ref, oa_ref, ob_ref, oc_ref, od_ref, gpre_ref, gpost_ref, wg_ref, wb_ref, wout_ref, o_ref):
    x = x_ref[...]
    tm = x.shape[1]
    xn = (_rms_rows(x) * _lanes(gpre_ref[...], tm)).astype(BF16)
    merged = jnp.zeros((D_MODEL, tm), F32)
    for n, br_ref in enumerate((oa_ref, ob_ref, oc_ref, od_ref)):
        gl = _dot(wg_ref[n * D_MODEL:(n + 1) * D_MODEL, :], xn)
        bp = _dot(wb_ref[n], br_ref[...])
        merged = merged + _sigmoid(gl) * bp
    mix = _dot(wout_ref[...], merged.astype(BF16))
    o_ref[...] = x + _rms_rows(mix) * _lanes(gpost_ref[...], tm)


def _merge_call(xT, branches, gb, wgT, wbT, woutT, l):
    t = xT.shape[1]
    const = lambda *idx: (lambda i: idx)
    tok = lambda rows: pl.BlockSpec((rows, TM), lambda i: (0, i))
    return pl.pallas_call(
        _merge_kernel,
        grid=(t // TM,),
        in_specs=[
            tok(D_MODEL), tok(BRANCH_W), tok(BRANCH_W), tok(BRANCH_W), tok(BRANCH_W),
            pl.BlockSpec((None, None, D_MODEL, LANES), const(l, 2, 0, 0)),
            pl.BlockSpec((None, None, D_MODEL, LANES), const(l, 3, 0, 0)),
            pl.BlockSpec((None, N_BRANCH * D_MODEL, D_MODEL), const(l, 0, 0), pipeline_mode=pl.Buffered(1)),
            pl.BlockSpec((None, N_BRANCH, D_MODEL, BRANCH_W), const(l, 0, 0, 0), pipeline_mode=pl.Buffered(1)),
            pl.BlockSpec((None, D_MODEL, D_MODEL), const(l, 0, 0), pipeline_mode=pl.Buffered(1)),
        ],
        out_specs=tok(D_MODEL),
        out_shape=jax.ShapeDtypeStruct(xT.shape, F32),
        compiler_params=pltpu.CompilerParams(dimension_semantics=("arbitrary",), vmem_limit_bytes=VMEM_LIMIT),
        name=f"merge_{l}",
    )(xT, *branches, gb, gb, wgT, wbT, woutT)


def _rope_tables(pos):
    def cs(p, dim):
        inv = 1.0 / (ROPE_THETA ** (jnp.arange(0, dim, 2, dtype=F32) / dim))
        ang = p[:, None] * inv[None, :]
        ang = jnp.concatenate([ang, ang], axis=-1)
        sign = jnp.concatenate([-jnp.ones((dim // 2,), F32), jnp.ones((dim // 2,), F32)])
        return jnp.cos(ang).T, (jnp.sin(ang) * sign[None, :]).T
    row = jnp.floor(pos / GRID_W)
    col = pos - row * GRID_W
    cr, sr = cs(row, HEAD_DIM // 2)
    cc, sc = cs(col, HEAD_DIM // 2)
    c64, s64 = cs(pos, HEAD_DIM)
    c32, s32 = cs(pos, C_ROPE)
    return (jnp.concatenate([cr, cc], axis=0), jnp.concatenate([sr, sc], axis=0), c64, s64, c32, s32)


def kernel(x_prompt, x_sample, norm_g, w_in, a_qk_norm, b_sink, c_q_norm, c_kv_norm, c_w_uq, c_w_ukv,
           d_lambda, d_subln, w_branch, w_out, ffn_wi, ffn_wo):
    b_p, s_p, _ = x_prompt.shape
    b_s, s_s, _ = x_sample.shape
    t_p, t_s = b_p * s_p, b_s * s_s
    t = t_p + t_s
    depth = norm_g.shape[0]
    seq_unit = max(TM, TM_FFN, TQC * N_CHAINS, PIPE_HS * PIPE_G * TK)
    assert s_p % seq_unit == 0 and s_s % seq_unit == 0
    seq = (t_p, s_p, s_s)

    xT = jnp.concatenate([x_prompt.reshape(t_p, D_MODEL), x_sample.reshape(t_s, D_MODEL)], axis=0).T

    pos = jnp.concatenate([jnp.tile(jnp.arange(s_p, dtype=F32), b_p), jnp.tile(jnp.arange(s_s, dtype=F32), b_s)])
    tabs = _rope_tables(pos)

    lane_bcast = lambda g: jnp.broadcast_to(g[..., None].astype(F32), g.shape + (LANES,))
    gb = lane_bcast(norm_g)
    aqkg = lane_bcast(a_qk_norm)
    cqg = lane_bcast(c_q_norm)
    ckvg = lane_bcast(c_kv_norm)
    dsg = lane_bcast(d_subln)
    tr = lambda w: jnp.swapaxes(w, -1, -2).astype(BF16)
    wiT, woT = tr(ffn_wi), tr(ffn_wo)
    winT, wgT = tr(w_in[:, :, :N_MIX_COLS]), tr(w_in[:, :, N_MIX_COLS:])
    wbT, woutT = tr(w_branch), tr(w_out)
    wuqT, wukvT = tr(c_w_uq), tr(c_w_ukv)
    sink = b_sink.astype(F32)
    dlam = d_lambda.astype(F32)

    nb = t // TK
    for l in range(depth):
        lambda_init = 0.8 - 0.6 * math.exp(-0.3 * l)
        xT = _ffn_call(xT, gb, wiT, woT, l, 0)
        (qa, ka, va, qb, kb, vb, qc, kc, vc, qd, kd, vd) = _prep_call(
            xT, gb, winT, aqkg, cqg, ckvg, wuqT, wukvT, tabs, l)
        k3 = lambda k: k.reshape(nb, TK, k.shape[1])
        oa = _dense_attn_call(qa, k3(ka), va, seq, n_steps=2, n_rows=2, k_block=lambda h: 0, name=f"attn_a_{l}")
        ob = _band_attn_call(sink[l], qb, k3(kb), vb, seq, l)
        oc = _dense_attn_call(qc, k3(kc), vc, seq, n_steps=4, n_rows=1, k_block=lambda h: h, name=f"attn_c_{l}")
        od = _dense_attn_call(qd, k3(kd), vd, seq, n_steps=4, n_rows=2, k_block=lambda h: h // 2,
                              name=f"attn_d_{l}", combine=True, extra=(dlam, dsg),
                              out_scale=1.0 - lambda_init, l=l)
        xT = _merge_call(xT, (oa, ob, oc, od), gb, wgT, wbT, woutT, l)
        xT = _ffn_call(xT, gb, wiT, woT, l, 1)

    y = xT.T
    return (y[:t_p].reshape(b_p, s_p, D_MODEL), y[t_p:].reshape(b_s, s_s, D_MODEL))
```
